```python
import jax, jax.numpy as jnp
from jax import lax
import numpy as np

D_MODEL = 2048
BATCH = 2
SEQ = 4096
DEPTH = 1
DEC_BATCH = 8
DEC_SEQ = 4
PAST_LEN = 16384
PAGE_SIZE = 128

D_MIX = D_MODEL
HEAD_DIM = 128
D_SB = D_MIX // 2
N_HEADS_SB = D_SB // HEAD_DIM
D_CM = D_MIX - D_SB
CM_GROUP = 128
N_GROUPS_CM = D_CM // CM_GROUP
CHUNK = 128
Q_BLOCK = 128
D_IN = 3 * D_SB + 2 * D_CM
SB_OFFSET_INIT = -6.0
D_PLE = 256
N_EXPERTS = 256
TOP_K = 8
N_EXPERT_GROUPS = 8
TOPK_GROUPS = 4
D_EXPERT = 512
D_SHARED = 512
ROUTE_SCALE = 2.5
DN_ALPHA = (2 * DEPTH) ** 0.25
DN_BETA = (8 * DEPTH) ** -0.25
LN_EPS = 1e-5
RMS_EPS = 1e-6

kernel_name = 'stickbreak_chunkmlp_moe_postnorm_step'


def layer_norm(x, g, b):
    xf = x.astype(jnp.float32)
    mu = jnp.mean(xf, -1, keepdims=True)
    var = jnp.mean(jnp.square(xf - mu), -1, keepdims=True)
    return ((xf - mu) * lax.rsqrt(var + LN_EPS) * g + b).astype(x.dtype)


def rms_norm(x, g):
    xf = x.astype(jnp.float32)
    return (xf * lax.rsqrt(jnp.mean(xf * xf, -1, keepdims=True) + RMS_EPS) * g).astype(x.dtype)


def project_in(x, w_in):
    B, S, _ = x.shape
    h = jnp.einsum('bsd,de->bse', x, w_in)
    q, k, v, u, vc = jnp.split(h, [D_SB, 2 * D_SB, 3 * D_SB, 3 * D_SB + D_CM], axis=-1)
    heads = lambda t: t.reshape(B, S, N_HEADS_SB, HEAD_DIM)
    return heads(q), heads(k), heads(v), u, vc


def stick_breaking_block(q, k, v, b_sb, q_pos, k_pos):
    z = jnp.einsum('bqhd,bkhd->bhqk', q, k).astype(jnp.float32) * (HEAD_DIM ** -0.5)
    z = z + b_sb.astype(jnp.float32)[None, :, None, None]
    valid = (k_pos[None, :] < q_pos[:, None])[None, None]
    log_keep = jnp.where(valid, -jax.nn.softplus(z), 0.0)
    log_keep_after = lax.cumsum(log_keep, axis=3, reverse=True) - log_keep
    weights = jnp.where(valid, jnp.exp(jax.nn.log_sigmoid(z) + log_keep_after), 0.0)
    return jnp.einsum('bhqk,bkhd->bqhd', weights.astype(v.dtype), v)


def stick_breaking_attention(q, k, v, b_sb, q_start):
    B, n, H, Dh = q.shape
    blk = min(Q_BLOCK, n)
    n_blk = -(-n // blk)
    qp = jnp.pad(q, ((0, 0), (0, n_blk * blk - n), (0, 0), (0, 0)))
    k_pos = jnp.arange(k.shape[1])

    def one_block(b):
        qb = lax.dynamic_slice_in_dim(qp, b * blk, blk, axis=1)
        return stick_breaking_block(qb, k, v, b_sb, q_start + b * blk + jnp.arange(blk), k_pos)

    out = lax.map(one_block, jnp.arange(n_blk))
    return out.transpose(1, 0, 2, 3, 4).reshape(B, n_blk * blk, H, Dh)[:, :n]


def chunk_mix(vn, w_s, b_s):
    B, n, G, C = vn.shape
    n_pad = -(-n // CHUNK) * CHUNK
    vp = jnp.pad(vn, ((0, 0), (0, n_pad - n), (0, 0), (0, 0))).reshape(B, n_pad // CHUNK, CHUNK, G, C)
    tri = jnp.tril(jnp.ones((CHUNK, CHUNK), dtype=bool))
    w = jnp.where(tri[None], w_s, 0.0).astype(vn.dtype)
    mixed = jnp.einsum('gts,bnsgc->bntgc', w, vp) + b_s.T[None, None, :, :, None]
    return mixed.reshape(B, n_pad, G, C)[:, :n]


def gating_unit(u, vc, ln_g, ln_b, w_s, b_s):
    B, S, _ = u.shape
    vg = jax.nn.gelu(vc).reshape(B, S, N_GROUPS_CM, CM_GROUP)
    vn = layer_norm(vg, ln_g.reshape(N_GROUPS_CM, CM_GROUP), ln_b.reshape(N_GROUPS_CM, CM_GROUP))
    mixed = chunk_mix(vn, w_s, b_s)
    return jax.nn.gelu(u) * mixed.reshape(B, S, D_CM), vn


def merge_out(o_sb, o_cm, g_sb, g_cm, w_out):
    B, S = o_cm.shape[:2]
    a = rms_norm(o_sb, g_sb.reshape(N_HEADS_SB, HEAD_DIM)).reshape(B, S, D_SB)
    c = rms_norm(o_cm.reshape(B, S, N_GROUPS_CM, CM_GROUP), g_cm.reshape(N_GROUPS_CM, CM_GROUP)).reshape(B, S, D_CM)
    return jnp.einsum('bse,ed->bsd', jnp.concatenate([a, c], -1), w_out)


def route(x2d, w_router, b_router):
    T = x2d.shape[0]
    scores = jax.nn.sigmoid((x2d @ w_router).astype(jnp.float32))
    sel = scores + b_router.astype(jnp.float32)
    per_group = sel.reshape(T, N_EXPERT_GROUPS, N_EXPERTS // N_EXPERT_GROUPS)
    group_score = lax.top_k(per_group, 2)[0].sum(-1)
    _, top_groups = lax.top_k(group_score, TOPK_GROUPS)
    gmask = jax.nn.one_hot(top_groups, N_EXPERT_GROUPS).sum(1) > 0
    emask = jnp.repeat(gmask, N_EXPERTS // N_EXPERT_GROUPS, axis=1)
    _, idx = lax.top_k(jnp.where(emask, sel, -jnp.inf), TOP_K)
    g = jnp.take_along_axis(scores, idx, axis=1)
    g = g / jnp.sum(g, -1, keepdims=True) * ROUTE_SCALE
    return idx, g


def routed_experts(x2d, idx, gates, layer, w_eg, w_eu, w_ed):
    T, D = x2d.shape
    tk = T * TOP_K
    blk = 128 if tk >= 32 * N_EXPERTS else 8
    flat_e = idx.reshape(-1)
    flat_tok = jnp.arange(tk, dtype=jnp.int32) // TOP_K
    flat_g = gates.reshape(-1)
    order = jnp.argsort(flat_e)
    e_sorted = flat_e[order]
    counts = jnp.bincount(flat_e, length=N_EXPERTS)
    padded = (counts + blk - 1) // blk * blk
    start = jnp.cumsum(counts) - counts
    pend = jnp.cumsum(padded)
    pstart = pend - padded
    dest = pstart[e_sorted] + jnp.arange(tk) - start[e_sorted]
    n_rows = (tk + N_EXPERTS * (blk - 1) + blk - 1) // blk * blk
    n_blocks = n_rows // blk
    row_tok = jnp.zeros((n_rows,), jnp.int32).at[dest].set(flat_tok[order])
    row_gate = jnp.zeros((n_rows,), jnp.float32).at[dest].set(flat_g[order])
    blk_e = jnp.clip(jnp.searchsorted(pend, jnp.arange(n_blocks) * blk, side='right'), 0, N_EXPERTS - 1)

    def one_block(args):
        toks, e = args
        xb = x2d[toks]
        h = jax.nn.silu(xb @ w_eg[layer, e]) * (xb @ w_eu[layer, e])
        return h @ w_ed[layer, e]

    yb = lax.map(one_block, (row_tok.reshape(n_blocks, blk), blk_e))
    y = yb.reshape(n_rows, D) * row_gate[:, None].astype(yb.dtype)
    return jnp.zeros((T, D), x2d.dtype).at[row_tok].add(y)


def moe(x, layer, w_router, b_router, w_eg, w_eu, w_ed, w_sg, w_su, w_sd):
    B, S, D = x.shape
    x2d = x.reshape(B * S, D)
    idx, g = route(x2d, w_router, b_router)
    routed = routed_experts(x2d, idx, g, layer, w_eg, w_eu, w_ed)
    shared = (jax.nn.silu(x2d @ w_sg) * (x2d @ w_su)) @ w_sd
    return (routed + shared).reshape(B, S, D)


def per_layer_embedding(x, p, w_gate, b_gate, w_proj, g_proj):
    gate = jax.nn.sigmoid(jnp.einsum('bsd,de->bse', x, w_gate) + b_gate)
    e = rms_norm(jnp.einsum('bsp,pd->bsd', p, w_proj), g_proj)
    return gate * e


def setup_inputs(seed: int = 0) -> dict:
    key = jax.random.key(seed)
    ks = jax.random.split(key, 33)
    n_pages = PAST_LEN // PAGE_SIZE
    n_used = DEC_BATCH * n_pages
    n_pool = n_used + (n_used + 3) // 4
    nrm = lambda k, shape, s: jax.random.normal(k, shape, jnp.float32) * s
    gain = lambda k, shape: 1.0 + nrm(k, shape, 0.02)
    page_table = jax.random.permutation(ks[6], n_pool)[:n_used].reshape(DEC_BATCH, n_pages).astype(jnp.int32)
    return {
        'x_prompt': nrm(ks[0], (BATCH, SEQ, D_MODEL), 1.0),
        'x_sample': nrm(ks[1], (DEC_BATCH, DEC_SEQ, D_MODEL), 1.0),
        'p_prompt': nrm(ks[2], (DEPTH, BATCH, SEQ, D_PLE), 1.0),
        'p_sample': nrm(ks[3], (DEPTH, DEC_BATCH, DEC_SEQ, D_PLE), 1.0),
        'cache_k': nrm(ks[4], (DEPTH, n_pool, PAGE_SIZE, N_HEADS_SB, HEAD_DIM), 1.0),
        'cache_v': nrm(ks[5], (DEPTH, n_pool, PAGE_SIZE, N_HEADS_SB, HEAD_DIM), 1.0),
        'page_table': page_table,
        'w_in': nrm(ks[7], (DEPTH, D_MODEL, D_IN), D_MODEL ** -0.5),
        'b_sb': SB_OFFSET_INIT + nrm(ks[32], (DEPTH, N_HEADS_SB), 0.1),
        'sg_ln_g': gain(ks[8], (DEPTH, D_CM)),
        'sg_ln_b': nrm(ks[9], (DEPTH, D_CM), 0.02),
        'w_spatial': nrm(ks[10], (DEPTH, N_GROUPS_CM, CHUNK, CHUNK), CHUNK ** -0.5),
        'b_spatial': gain(ks[11], (DEPTH, N_GROUPS_CM, CHUNK)),
        'g_out_sb': gain(ks[12], (DEPTH, D_SB)),
        'g_out_cm': gain(ks[13], (DEPTH, D_CM)),
        'w_out': nrm(ks[14], (DEPTH, D_MIX, D_MODEL), D_MIX ** -0.5 * DN_BETA),
        'ln1_g': gain(ks[15], (DEPTH, D_MODEL)),
        'ln1_b': nrm(ks[16], (DEPTH, D_MODEL), 0.02),
        'w_router': nrm(ks[17], (DEPTH, D_MODEL, N_EXPERTS), D_MODEL ** -0.5),
        'b_router': nrm(ks[18], (DEPTH, N_EXPERTS), 0.01),
        'w_exp_gate': nrm(ks[19], (DEPTH, N_EXPERTS, D_MODEL, D_EXPERT), D_MODEL ** -0.5),
        'w_exp_up': nrm(ks[20], (DEPTH, N_EXPERTS, D_MODEL, D_EXPERT), D_MODEL ** -0.5),
        'w_exp_down': nrm(ks[21], (DEPTH, N_EXPERTS, D_EXPERT, D_MODEL), D_EXPERT ** -0.5 * DN_BETA),
        'w_sh_gate': nrm(ks[22], (DEPTH, D_MODEL, D_SHARED), D_MODEL ** -0.5),
        'w_sh_up': nrm(ks[23], (DEPTH, D_MODEL, D_SHARED), D_MODEL ** -0.5),
        'w_sh_down': nrm(ks[24], (DEPTH, D_SHARED, D_MODEL), D_SHARED ** -0.5 * DN_BETA),
        'ln2_g': gain(ks[25], (DEPTH, D_MODEL)),
        'ln2_b': nrm(ks[26], (DEPTH, D_MODEL), 0.02),
        'w_ple_gate': nrm(ks[27], (DEPTH, D_MODEL, D_MODEL), D_MODEL ** -0.5),
        'b_ple_gate': nrm(ks[28], (DEPTH, D_MODEL), 0.02),
        'w_ple_proj': nrm(ks[29], (DEPTH, D_PLE, D_MODEL), D_PLE ** -0.5),
        'g_ple': gain(ks[30], (DEPTH, D_MODEL)),
    }


def reference(x_prompt, x_sample, p_prompt, p_sample, cache_k, cache_v, page_table,
              w_in, b_sb, sg_ln_g, sg_ln_b, w_spatial, b_spatial, g_out_sb, g_out_cm, w_out,
              ln1_g, ln1_b, w_router, b_router, w_exp_gate, w_exp_up, w_exp_down,
              w_sh_gate, w_sh_up, w_sh_down, ln2_g, ln2_b,
              w_ple_gate, b_ple_gate, w_ple_proj, g_ple):

    def layer(i, x, p, attend):
        q, k, v, u, vc = project_in(x, w_in[i])
        o_sb = attend(i, q, k, v)
        o_cm, vn = gating_unit(u, vc, sg_ln_g[i], sg_ln_b[i], w_spatial[i], b_spatial[i])
        x = layer_norm(DN_ALPHA * x + merge_out(o_sb, o_cm, g_out_sb[i], g_out_cm[i], w_out[i]),
                       ln1_g[i], ln1_b[i])
        x = layer_norm(DN_ALPHA * x + moe(x, i, w_router[i], b_router[i], w_exp_gate, w_exp_up,
                                          w_exp_down, w_sh_gate[i], w_sh_up[i], w_sh_down[i]),
                       ln2_g[i], ln2_b[i])
        x = x + per_layer_embedding(x, p, w_ple_gate[i], b_ple_gate[i], w_ple_proj[i], g_ple[i])
        return x, k, v, vn

    def attend_prompt(i, q, k, v):
        return stick_breaking_attention(q, k, v, b_sb[i], 0)

    def attend_sample(i, q, k, v):
        DB = q.shape[0]
        past_k = cache_k[i][page_table].reshape(DB, -1, N_HEADS_SB, HEAD_DIM)
        past_v = cache_v[i][page_table].reshape(DB, -1, N_HEADS_SB, HEAD_DIM)
        keys = jnp.concatenate([past_k, k], axis=1)
        vals = jnp.concatenate([past_v, v], axis=1)
        return stick_breaking_attention(q, keys, vals, b_sb[i], past_k.shape[1])

    xp, xs = x_prompt, x_sample
    kp_l, vp_l, ks_l, vs_l, cs_l = [], [], [], [], []
    for i in range(DEPTH):
        xp, kp, vp, _ = layer(i, xp, p_prompt[i], attend_prompt)
        xs, kss, vss, vns = layer(i, xs, p_sample[i], attend_sample)
        kp_l.append(kp)
        vp_l.append(vp)
        ks_l.append(kss)
        vs_l.append(vss)
        cs_l.append(vns)
    k_prompt = jnp.stack(kp_l, 0)
    v_prompt = jnp.stack(vp_l, 0)
    k_sample = jnp.stack(ks_l, 0)
    v_sample = jnp.stack(vs_l, 0)
    chunk_v_sample = jnp.stack(cs_l, 0)
    return (xp, xs, k_prompt, v_prompt, k_sample, v_sample, chunk_v_sample)
```

```python
import functools

import jax
import jax.numpy as jnp
from jax import lax
from jax.experimental import pallas as pl
from jax.experimental.pallas import tpu as pltpu

TOP_K = 8
N_EXPERT_GROUPS = 8
TOPK_GROUPS = 4
ROUTE_SCALE = 2.5
LN_EPS = 1e-5
RMS_EPS = 1e-6

LANES = 128
SUBLANES = 8
VMEM_LIMIT_BYTES = 56 * 1024 * 1024

EXPERT_ROW_BLOCK = 256
F32 = jnp.float32
BF16 = jnp.bfloat16
NEG_INF = float("-inf")


def _cparams(*sem):
    return pltpu.CompilerParams(dimension_semantics=sem, vmem_limit_bytes=VMEM_LIMIT_BYTES)


def _dot(a, b, precise):
    if precise:
        return jnp.dot(a.astype(F32), b.astype(F32), precision=lax.Precision.HIGHEST,
                       preferred_element_type=F32)
    return jnp.dot(a.astype(BF16), b.astype(BF16), preferred_element_type=F32)


def _dot_nt(a, b):
    return lax.dot_general(a, b, (((1,), (1,)), ((), ())), preferred_element_type=F32)


def _split(x):
    hi = x.astype(BF16)
    lo = (x - hi.astype(F32)).astype(BF16)
    return hi, lo


def _gelu(x):
    c = 0.7978845608028654
    return x * (0.5 * (1.0 + jnp.tanh(c * (x + 0.044715 * (x * x * x)))))


def _strict_upper(n):
    r = lax.broadcasted_iota(jnp.int32, (n, n), 0)
    c = lax.broadcasted_iota(jnp.int32, (n, n), 1)
    return (r > c).astype(BF16)


def _stick_terms(z):
    l1p = jnp.log1p(jnp.exp(-jnp.abs(z)))
    log_keep = -(jnp.maximum(z, 0.0) + l1p)
    log_beta = jnp.minimum(z, 0.0) - l1p
    return log_keep, log_beta


def _suffix_sums(log_keep, upper):
    hi, lo = _split(log_keep)
    return (jnp.dot(hi, upper, preferred_element_type=F32)
            + jnp.dot(lo, upper, preferred_element_type=F32))


def _proj_in_kernel(x_ref, w_ref, lng_ref, lnb_ref, q_ref, k_ref, v_ref, u_ref, vn_ref, *,
                    precise, n_per, group):
    seg = pl.program_id(1) // n_per
    acc = _dot(x_ref[...], w_ref[...], precise)

    @pl.when(seg == 0)
    def _():
        q_ref[...] = acc.astype(q_ref.dtype)

    @pl.when(seg == 1)
    def _():
        k_ref[...] = acc.astype(k_ref.dtype)

    @pl.when(seg == 2)
    def _():
        v_ref[...] = acc.astype(v_ref.dtype)

    @pl.when(seg == 3)
    def _():
        u_ref[...] = _gelu(acc).astype(u_ref.dtype)

    @pl.when(seg == 4)
    def _():
        g = _gelu(acc)
        for c in range(acc.shape[1] // group):
            sl = slice(c * group, (c + 1) * group)
            xg = g[:, sl]
            mu = jnp.mean(xg, axis=-1, keepdims=True)
            xc = xg - mu
            var = jnp.mean(xc * xc, axis=-1, keepdims=True)
            y = xc * lax.rsqrt(var + LN_EPS) * lng_ref[:, sl] + lnb_ref[:, sl]
            vn_ref[:, sl] = y.astype(vn_ref.dtype)


def _proj_in(x2d, w, ln_g, ln_b, *, width, group, precise, tm, tn, dtypes):
    t, d = x2d.shape
    n_per = width // tn
    grid = (t // tm, 5 * n_per)

    def out_map(o):
        return lambda i, j: (i, jnp.clip(j - o * n_per, 0, n_per - 1))

    ln_map = lambda i, j: (0, jnp.clip(j - 4 * n_per, 0, n_per - 1))
    return pl.pallas_call(
        functools.partial(_proj_in_kernel, precise=precise, n_per=n_per, group=group),
        grid=grid,
        in_specs=[pl.BlockSpec((tm, d), lambda i, j: (i, 0)),
                  pl.BlockSpec((d, tn), lambda i, j: (0, j)),
                  pl.BlockSpec((1, tn), ln_map),
                  pl.BlockSpec((1, tn), ln_map)],
        out_specs=[pl.BlockSpec((tm, tn), out_map(o)) for o in range(5)],
        out_shape=[jax.ShapeDtypeStruct((t, width), dt) for dt in dtypes],
        compiler_params=_cparams("parallel", "arbitrary"),
        name="proj_in",
    )(x2d, w, ln_g, ln_b)


def _attn_prompt_kernel(bsb_ref, q_ref, k_ref, v_ref, o_ref, kb_ref, vb_ref, *, tq, scale):
    h = pl.program_id(1)
    i = pl.program_id(2)

    @pl.when(i == 0)
    def _():
        kb_ref[...] = k_ref[...].astype(BF16)
        vb_ref[...] = v_ref[...].astype(BF16)

    bias = bsb_ref[h]
    q = q_ref[...]
    upper = _strict_upper(tq)
    row = lax.broadcasted_iota(jnp.int32, (tq, tq), 0)
    col = lax.broadcasted_iota(jnp.int32, (tq, tq), 1)
    causal = col < row

    def block(kb, acc, carry, masked):
        start = pl.multiple_of(kb * tq, tq)
        k_blk = kb_ref[pl.ds(start, tq), :]
        v_blk = vb_ref[pl.ds(start, tq), :]
        z = _dot_nt(q, k_blk) * scale + bias
        log_keep, log_beta = _stick_terms(z)
        if masked:
            log_keep = jnp.where(causal, log_keep, 0.0)
        after = _suffix_sums(log_keep, upper)
        w = jnp.exp(log_beta + after + carry)
        if masked:
            w = jnp.where(causal, w, 0.0)
        acc = acc + jnp.dot(w.astype(BF16), v_blk, preferred_element_type=F32)
        carry = carry + jnp.sum(log_keep, axis=-1, keepdims=True)
        return acc, carry

    acc0 = jnp.zeros((tq, q.shape[1]), F32)
    carry0 = jnp.zeros((tq, 1), F32)
    acc, carry = block(i, acc0, carry0, True)
    acc, carry = lax.fori_loop(0, i, lambda jj, c: block(i - 1 - jj, c[0], c[1], False), (acc, carry))
    o_ref[...] = acc.astype(o_ref.dtype)


def _attn_prompt(q, k, v, b_sb, *, batch, seq, heads, tq, out_dtype):
    t, width = q.shape
    dh = width // heads
    nq = seq // tq
    grid = (batch, heads, nq)
    return pl.pallas_call(
        functools.partial(_attn_prompt_kernel, tq=tq, scale=dh ** -0.5),
        grid_spec=pltpu.PrefetchScalarGridSpec(
            num_scalar_prefetch=0,
            grid=grid,
            in_specs=[pl.BlockSpec(memory_space=pltpu.SMEM),
                      pl.BlockSpec((tq, dh), lambda b, h, i: (b * nq + i, h)),
                      pl.BlockSpec((seq, dh), lambda b, h, i: (b, h)),
                      pl.BlockSpec((seq, dh), lambda b, h, i: (b, h))],
            out_specs=pl.BlockSpec((tq, dh), lambda b, h, i: (b * nq + i, h)),
            scratch_shapes=[pltpu.VMEM((seq, dh), BF16), pltpu.VMEM((seq, dh), BF16)]),
        out_shape=jax.ShapeDtypeStruct((t, width), out_dtype),
        compiler_params=_cparams("arbitrary", "arbitrary", "arbitrary"),
        name="attn_prompt",
    )(b_sb, q, k, v)


def _attn_decode_kernel(pt_ref, bsb_ref, q_ref, kn_ref, vn_ref, kp_ref, vp_ref, o_ref, carry_ref, *,
                        heads, rows, page, scale):
    p = pl.program_id(1)
    upper = _strict_upper(page)
    n = heads * rows
    row = lax.broadcasted_iota(jnp.int32, (n, page), 0)
    col = lax.broadcasted_iota(jnp.int32, (n, page), 1)
    causal = col < (row % rows)
    bias = jnp.concatenate([jnp.full((rows, page), bsb_ref[h], F32) for h in range(heads)], axis=0)
    q_hi, q_lo = _split(q_ref[0])

    def block(k_ref, v_ref, masked):
        zs = []
        for h in range(heads):
            k_hi, k_lo = _split(k_ref[0, pl.ds(h, page, stride=heads), :])
            qh, ql = q_hi[h * rows:(h + 1) * rows], q_lo[h * rows:(h + 1) * rows]
            zs.append(_dot_nt(qh, k_hi) + _dot_nt(ql, k_hi) + _dot_nt(qh, k_lo))
        z = jnp.concatenate(zs, axis=0) * scale + bias
        log_keep, log_beta = _stick_terms(z)
        if masked:
            log_keep = jnp.where(causal, log_keep, 0.0)
        after = _suffix_sums(log_keep, upper)
        w = jnp.exp(log_beta + after + carry_ref[...])
        if masked:
            w = jnp.where(causal, w, 0.0)
        w_hi, w_lo = _split(w)
        for h in range(heads):
            v_hi, v_lo = _split(v_ref[0, pl.ds(h, page, stride=heads), :])
            sl = slice(h * rows, (h + 1) * rows)
            o_ref[0, sl, :] += (jnp.dot(w_hi[sl], v_hi, preferred_element_type=F32)
                                + jnp.dot(w_lo[sl], v_hi, preferred_element_type=F32)
                                + jnp.dot(w_hi[sl], v_lo, preferred_element_type=F32))
        carry_ref[...] += jnp.sum(log_keep, axis=-1, keepdims=True)

    @pl.when(p == 0)
    def _():
        o_ref[...] = jnp.zeros_like(o_ref)
        carry_ref[...] = jnp.zeros_like(carry_ref)
        block(kn_ref, vn_ref, True)

    block(kp_ref, vp_ref, False)


def _attn_decode(q_rows, k_new, v_new, k_pool, v_pool, page_table, b_sb, *, heads, rows, page):
    db, n, dh = q_rows.shape
    n_pages = page_table.shape[1]
    blk = (1, page * heads, dh)
    return pl.pallas_call(
        functools.partial(_attn_decode_kernel, heads=heads, rows=rows, page=page, scale=dh ** -0.5),
        grid_spec=pltpu.PrefetchScalarGridSpec(
            num_scalar_prefetch=2,
            grid=(db, n_pages),
            in_specs=[pl.BlockSpec((1, n, dh), lambda b, p, pt, bs: (b, 0, 0)),
                      pl.BlockSpec(blk, lambda b, p, pt, bs: (b, 0, 0)),
                      pl.BlockSpec(blk, lambda b, p, pt, bs: (b, 0, 0)),
                      pl.BlockSpec(blk, lambda b, p, pt, bs: (pt[b, n_pages - 1 - p], 0, 0)),
                      pl.BlockSpec(blk, lambda b, p, pt, bs: (pt[b, n_pages - 1 - p], 0, 0))],
            out_specs=pl.BlockSpec((1, n, dh), lambda b, p, pt, bs: (b, 0, 0)),
            scratch_shapes=[pltpu.VMEM((n, LANES), F32)]),
        out_shape=jax.ShapeDtypeStruct((db, n, dh), F32),
        compiler_params=_cparams("parallel", "arbitrary"),
        name="attn_decode",
    )(page_table, b_sb, q_rows, k_new, v_new, k_pool, v_pool)


def _merge_kernel(osb_ref, u_ref, vn_ref, x_ref, wm_ref, bm_ref, gsb_ref, gcm_ref, wout_ref,
                  lng_ref, lnb_ref, x1_ref, ac_ref, *, precise, alpha, heads, groups, dh, cg, mix_rows):
    tm = x_ref.shape[0]
    d_sb = heads * dh
    for h in range(heads):
        sl = slice(h * dh, (h + 1) * dh)
        o = osb_ref[:, sl].astype(F32)
        ms = jnp.mean(o * o, axis=-1, keepdims=True)
        ac_ref[:, sl] = (o * lax.rsqrt(ms + RMS_EPS) * gsb_ref[:, sl]).astype(ac_ref.dtype)
    for r in range(tm // mix_rows):
        rs = slice(r * mix_rows, (r + 1) * mix_rows)
        for g in range(groups):
            sl = slice(g * cg, (g + 1) * cg)
            mixed = _dot(wm_ref[g], vn_ref[rs, sl], precise) + bm_ref[g]
            ocm = u_ref[rs, sl].astype(F32) * mixed
            ms = jnp.mean(ocm * ocm, axis=-1, keepdims=True)
            ac_ref[rs, d_sb + g * cg:d_sb + (g + 1) * cg] = (
                ocm * lax.rsqrt(ms + RMS_EPS) * gcm_ref[:, sl]).astype(ac_ref.dtype)
    y = alpha * x_ref[...] + _dot(ac_ref[...], wout_ref[...], precise)
    mu = jnp.mean(y, axis=-1, keepdims=True)
    yc = y - mu
    var = jnp.mean(yc * yc, axis=-1, keepdims=True)
    x1_ref[...] = yc * lax.rsqrt(var + LN_EPS) * lng_ref[...] + lnb_ref[...]


def _merge(o_sb, u_act, vn, x2d, wm, bm, g_sb, g_cm, w_out, ln_g, ln_b, *, precise, alpha, heads,
           groups, tm):
    t, d = x2d.shape
    d_sb, d_cm = o_sb.shape[1], u_act.shape[1]
    mix_rows = wm.shape[1]
    full = lambda a: pl.BlockSpec(a.shape, lambda i: (0,) * a.ndim)
    rows = lambda a: pl.BlockSpec((tm, a.shape[1]), lambda i: (i, 0))
    return pl.pallas_call(
        functools.partial(_merge_kernel, precise=precise, alpha=alpha, heads=heads, groups=groups,
                          dh=d_sb // heads, cg=d_cm // groups, mix_rows=mix_rows),
        grid=(t // tm,),
        in_specs=[rows(o_sb), rows(u_act), rows(vn), rows(x2d), full(wm), full(bm), full(g_sb),
                  full(g_cm), full(w_out), full(ln_g), full(ln_b)],
        out_specs=pl.BlockSpec((tm, d), lambda i: (i, 0)),
        out_shape=jax.ShapeDtypeStruct((t, d), F32),
        scratch_shapes=[pltpu.VMEM((tm, d_sb + d_cm), F32 if precise else BF16)],
        compiler_params=_cparams("parallel"),
        name="merge_out",
    )(o_sb, u_act, vn, x2d, wm, bm, g_sb, g_cm, w_out, ln_g, ln_b)


def _first_max(vals, index, size, axis):
    m = jnp.max(vals, axis=axis, keepdims=True)
    ix = jnp.min(jnp.where(vals == m, index, size), axis=axis, keepdims=True)
    return m, ix


def _router_kernel(x_ref, wrt_ref, br_ref, idx_ref, gate_ref, *, precise):
    n_e = wrt_ref.shape[0]
    tm = x_ref.shape[0]
    eg = n_e // N_EXPERT_GROUPS
    if precise:
        x_hi, x_lo = _split(x_ref[...])
        w_hi, w_lo = _split(wrt_ref[...])
        logits = _dot_nt(w_hi, x_hi) + _dot_nt(w_lo, x_hi) + _dot_nt(w_hi, x_lo)
    else:
        logits = _dot_nt(wrt_ref[...].astype(BF16), x_ref[...].astype(BF16))
    scores = jax.nn.sigmoid(logits)
    sel = scores + br_ref[:, :tm]
    pos = lax.broadcasted_iota(jnp.int32, (eg, tm), 0)
    sel_g, gscore = [], []
    for g in range(N_EXPERT_GROUPS):
        sg = sel[g * eg:(g + 1) * eg]
        m1, i1 = _first_max(sg, pos, eg, 0)
        m2 = jnp.max(jnp.where(pos == i1, NEG_INF, sg), axis=0, keepdims=True)
        sel_g.append(sg)
        gscore.append(m1 + m2)
    cand = []
    for g in range(N_EXPERT_GROUPS):
        beaten = jnp.zeros((1, tm), jnp.int32)
        for o in range(N_EXPERT_GROUPS):
            if o != g:
                wins = (gscore[o] >= gscore[g]) if o < g else (gscore[o] > gscore[g])
                beaten = beaten + wins.astype(jnp.int32)
        cand.append(jnp.where(beaten < TOPK_GROUPS, sel_g[g], NEG_INF))
    cand = jnp.concatenate(cand, axis=0)
    epos = lax.broadcasted_iota(jnp.int32, cand.shape, 0)
    gs = []
    for k in range(TOP_K):
        _, ei = _first_max(cand, epos, n_e, 0)
        pick = epos == ei
        idx_ref[k:k + 1, :] = ei
        gs.append(jnp.sum(jnp.where(pick, scores, 0.0), axis=0, keepdims=True))
        cand = jnp.where(pick, NEG_INF, cand)
    total = gs[0]
    for k in range(1, TOP_K):
        total = total + gs[k]
    for k in range(TOP_K):
        gate_ref[k:k + 1, :] = gs[k] / total * ROUTE_SCALE


def _router(x1, wr_t, br, *, precise, tm):
    t, d = x1.shape
    n_e = wr_t.shape[0]
    return pl.pallas_call(
        functools.partial(_router_kernel, precise=precise),
        grid=(t // tm,),
        in_specs=[pl.BlockSpec((tm, d), lambda i: (i, 0)),
                  pl.BlockSpec((n_e, d), lambda i: (0, 0)),
                  pl.BlockSpec(br.shape, lambda i: (0, 0))],
        out_specs=[pl.BlockSpec((TOP_K, tm), lambda i: (0, i)),
                   pl.BlockSpec((TOP_K, tm), lambda i: (0, i))],
        out_shape=[jax.ShapeDtypeStruct((TOP_K, t), jnp.int32),
                   jax.ShapeDtypeStruct((TOP_K, t), F32)],
        compiler_params=_cparams("parallel"),
        name="router",
    )(x1, wr_t, br)


def _row_gather_copy(src_hbm, dst, sem, src_row, dst_row):
    return pltpu.make_async_copy(src_hbm.at[pl.ds(src_row, 1)], dst.at[pl.ds(dst_row, 1)], sem)


def _expert_kernel(blk_e_ref, nused_ref, tok_ref, tok_next_ref, x_hbm, wg_ref, wu_ref, wd_ref, y_ref,
                   xbuf, wgb, wub, wdb, sem):
    s = pl.program_id(0)
    n_used = nused_ref[0]
    blk = xbuf.shape[1]
    slot = s % 2

    def issue(ids_ref, to_slot):
        def body(r, carry):
            _row_gather_copy(x_hbm, xbuf.at[to_slot], sem.at[to_slot], ids_ref[0, 0, r], r).start()
            return carry
        lax.fori_loop(0, blk, body, 0)

    @pl.when(s == 0)
    def _():
        issue(tok_ref, 0)

    @pl.when(s + 1 < n_used)
    def _():
        issue(tok_next_ref, 1 - slot)

    @pl.when(s < n_used)
    def _():
        e_now = blk_e_ref[s]
        e_prev = blk_e_ref[jnp.maximum(s - 1, 0)]

        @pl.when(jnp.logical_or(s == 0, e_now != e_prev))
        def _():
            wgb[...] = wg_ref[...].astype(BF16)
            wub[...] = wu_ref[...].astype(BF16)
            wdb[...] = wd_ref[...].astype(BF16)

        pltpu.make_async_copy(x_hbm.at[pl.ds(0, blk)], xbuf.at[slot], sem.at[slot]).wait()
        x = xbuf[slot].astype(BF16)
        gate = jnp.dot(x, wgb[...], preferred_element_type=F32)
        up = jnp.dot(x, wub[...], preferred_element_type=F32)
        hid = (gate * jax.nn.sigmoid(gate)) * up
        y_ref[...] = jnp.dot(hid.astype(BF16), wdb[...], preferred_element_type=F32)


def _experts(x_all, row_tok, blk_e, n_used, w_eg, w_eu, w_ed, layer, *, blk):
    n_rows = row_tok.shape[0]
    n_blocks = n_rows // blk
    d = x_all.shape[1]
    de = w_eg.shape[3]
    tok3 = row_tok.reshape(n_blocks, 1, blk)
    last = lambda s, nu: jnp.minimum(s, nu[0] - 1)
    wspec = lambda a: pl.BlockSpec((None, None) + a.shape[2:], lambda s, be, nu: (layer, be[s], 0, 0))
    return pl.pallas_call(
        _expert_kernel,
        grid_spec=pltpu.PrefetchScalarGridSpec(
            num_scalar_prefetch=2,
            grid=(n_blocks,),
            in_specs=[pl.BlockSpec((1, 1, blk), lambda s, be, nu: (last(s, nu), 0, 0),
                                   memory_space=pltpu.SMEM),
                      pl.BlockSpec((1, 1, blk), lambda s, be, nu: (last(s + 1, nu), 0, 0),
                                   memory_space=pltpu.SMEM),
                      pl.BlockSpec(memory_space=pl.ANY),
                      wspec(w_eg), wspec(w_eu), wspec(w_ed)],
            out_specs=pl.BlockSpec((blk, d), lambda s, be, nu: (last(s, nu), 0)),
            scratch_shapes=[pltpu.VMEM((2, blk, d), F32),
                            pltpu.VMEM((d, de), BF16), pltpu.VMEM((d, de), BF16),
                            pltpu.VMEM((de, d), BF16),
                            pltpu.SemaphoreType.DMA((2,))]),
        out_shape=jax.ShapeDtypeStruct((n_rows, d), F32),
        compiler_params=_cparams("arbitrary"),
        name="experts",
    )(blk_e, n_used, tok3, tok3, x_all, w_eg, w_eu, w_ed)


def _combine_kernel(dest_ref, dest_next_ref, gate_ref, x1_ref, ys_hbm, wsg_ref, wsu_ref, wsd_ref,
                    lng_ref, lnb_ref, x2_ref, ybuf, sem, *, precise, alpha):
    i = pl.program_id(0)
    n_steps = pl.num_programs(0)
    tm = x1_ref.shape[0]
    slot = i % 2

    def issue(ids_ref, to_slot):
        def body(r, carry):
            for k in range(TOP_K):
                _row_gather_copy(ys_hbm, ybuf.at[to_slot, k], sem.at[to_slot], ids_ref[0, k, r], r).start()
            return carry
        lax.fori_loop(0, tm, body, 0)

    @pl.when(i == 0)
    def _():
        issue(dest_ref, 0)

    @pl.when(i + 1 < n_steps)
    def _():
        issue(dest_next_ref, 1 - slot)

    x1 = x1_ref[...]
    hid = jax.nn.silu(_dot(x1, wsg_ref[...], precise)) * _dot(x1, wsu_ref[...], precise)
    y = alpha * x1 + _dot(hid, wsd_ref[...], precise)
    for k in range(TOP_K):
        pltpu.make_async_copy(ys_hbm.at[pl.ds(0, tm)], ybuf.at[slot, k], sem.at[slot]).wait()
    for k in range(TOP_K):
        y = y + gate_ref[:, k:k + 1] * ybuf[slot, k]
    mu = jnp.mean(y, axis=-1, keepdims=True)
    yc = y - mu
    var = jnp.mean(yc * yc, axis=-1, keepdims=True)
    x2_ref[...] = yc * lax.rsqrt(var + LN_EPS) * lng_ref[...] + lnb_ref[...]


def _combine(dest_t, gates, x1, ys, wsg, wsu, wsd, ln_g, ln_b, *, precise, alpha, tm):
    t, d = x1.shape
    n_steps = t // tm
    dest3 = dest_t.reshape(TOP_K, n_steps, tm).transpose(1, 0, 2)
    full = lambda a: pl.BlockSpec(a.shape, lambda i: (0,) * a.ndim)
    return pl.pallas_call(
        functools.partial(_combine_kernel, precise=precise, alpha=alpha),
        grid=(n_steps,),
        in_specs=[pl.BlockSpec((1, TOP_K, tm), lambda i: (i, 0, 0), memory_space=pltpu.SMEM),
                  pl.BlockSpec((1, TOP_K, tm), lambda i: (jnp.minimum(i + 1, n_steps - 1), 0, 0),
                               memory_space=pltpu.SMEM),
                  pl.BlockSpec((tm, TOP_K), lambda i: (i, 0)),
                  pl.BlockSpec((tm, d), lambda i: (i, 0)),
                  pl.BlockSpec(memory_space=pl.ANY),
                  full(wsg), full(wsu), full(wsd), full(ln_g), full(ln_b)],
        out_specs=pl.BlockSpec((tm, d), lambda i: (i, 0)),
        out_shape=jax.ShapeDtypeStruct((t, d), F32),
        scratch_shapes=[pltpu.VMEM((2, TOP_K, tm, d), F32), pltpu.SemaphoreType.DMA((2,))],
        compiler_params=_cparams("arbitrary"),
        name="combine",
    )(dest3, dest3, gates, x1, ys, wsg, wsu, wsd, ln_g, ln_b)


def _ple_kernel(x_ref, xcol_ref, p_ref, wg_ref, bg_ref, wp_ref, gp_ref, y_ref, e_ref, *, precise):
    j = pl.program_id(1)
    tn = y_ref.shape[1]

    @pl.when(j == 0)
    def _():
        e = _dot(p_ref[...], wp_ref[...], precise)
        ms = jnp.mean(e * e, axis=-1, keepdims=True)
        e = e * lax.rsqrt(ms + RMS_EPS) * gp_ref[...]
        for jj in range(e_ref.shape[0]):
            e_ref[jj] = e[:, jj * tn:(jj + 1) * tn]

    gate = jax.nn.sigmoid(_dot(x_ref[...], wg_ref[...], precise) + bg_ref[...])
    y_ref[...] = xcol_ref[...] + gate * e_ref[j]


def _ple(x2, p2d, w_gate, b_gate, w_proj, g_proj, *, precise, tm, tn):
    t, d = x2.shape
    dp = p2d.shape[1]
    return pl.pallas_call(
        functools.partial(_ple_kernel, precise=precise),
        grid=(t // tm, d // tn),
        in_specs=[pl.BlockSpec((tm, d), lambda i, j: (i, 0)),
                  pl.BlockSpec((tm, tn), lambda i, j: (i, j)),
                  pl.BlockSpec((tm, dp), lambda i, j: (i, 0)),
                  pl.BlockSpec((d, tn), lambda i, j: (0, j)),
                  pl.BlockSpec((1, tn), lambda i, j: (0, j)),
                  pl.BlockSpec((dp, d), lambda i, j: (0, 0)),
                  pl.BlockSpec((1, d), lambda i, j: (0, 0))],
        out_specs=pl.BlockSpec((tm, tn), lambda i, j: (i, j)),
        out_shape=jax.ShapeDtypeStruct((t, d), F32),
        scratch_shapes=[pltpu.VMEM((d // tn, tm, tn), F32)],
        compiler_params=_cparams("parallel", "arbitrary"),
        name="ple",
    )(x2, x2, p2d, w_gate, b_gate, w_proj, g_proj)


def _dispatch_plan(idx, n_experts, blk):
    t, k = idx.shape
    tk = t * k
    flat_e = idx.reshape(-1)
    order = jnp.argsort(flat_e)
    e_sorted = flat_e[order]
    counts = jnp.bincount(flat_e, length=n_experts)
    padded = (counts + blk - 1) // blk * blk
    start = jnp.cumsum(counts) - counts
    pend = jnp.cumsum(padded)
    pstart = pend - padded
    dest_sorted = (pstart[e_sorted] + jnp.arange(tk) - start[e_sorted]).astype(jnp.int32)
    n_rows = (tk + n_experts * (blk - 1) + blk - 1) // blk * blk
    n_blocks = n_rows // blk
    row_tok = jnp.zeros((n_rows,), jnp.int32).at[dest_sorted].set((order // k).astype(jnp.int32))
    dest = jnp.zeros((tk,), jnp.int32).at[order].set(dest_sorted).reshape(t, k)
    n_used = (pend[-1] // blk).astype(jnp.int32)
    blk_e = jnp.clip(jnp.searchsorted(pend, jnp.arange(n_blocks) * blk, side="right"), 0, n_experts - 1)
    blk_e = jnp.where(jnp.arange(n_blocks) < n_used, blk_e, blk_e[jnp.maximum(n_used - 1, 0)])
    return row_tok, dest, blk_e.astype(jnp.int32), n_used.reshape(1)


def _tile(n, want):
    t = min(n, want)
    while n % t:
        t -= 1
    return t


def kernel(x_prompt, x_sample, p_prompt, p_sample, cache_k, cache_v, page_table, w_in, b_sb, sg_ln_g, sg_ln_b, w_spatial, b_spatial, g_out_sb, g_out_cm, w_out, ln1_g, ln1_b, w_router, b_router, w_exp_gate, w_exp_up, w_exp_down, w_sh_gate, w_sh_up, w_sh_down, ln2_g, ln2_b, w_ple_gate, b_ple_gate, w_ple_proj, g_ple):
    batch, seq, d = x_prompt.shape
    db, ds, _ = x_sample.shape
    depth = w_in.shape[0]
    n_pool, page, heads, dh = cache_k.shape[1:]
    d_sb = heads * dh
    d_cm = sg_ln_g.shape[1]
    groups, chunk = w_spatial.shape[1], w_spatial.shape[2]
    cg = d_cm // groups
    n_e = w_router.shape[2]
    alpha = float((2 * depth) ** 0.25)
    tp, ts = batch * seq, db * ds
    assert d_sb == d_cm and w_in.shape[2] == 3 * d_sb + 2 * d_cm
    assert seq % chunk == 0 and ds <= min(chunk, SUBLANES) and page == chunk == LANES

    row = lambda a: a.reshape(1, -1)
    tril = jnp.tril(jnp.ones((chunk, chunk), jnp.bool_))
    xp = x_prompt.reshape(tp, d)
    xs = x_sample.reshape(ts, d)
    outs = {k: [] for k in ("kp", "vp", "ks", "vs", "cs")}

    for li in range(depth):
        lng, lnb = row(sg_ln_g[li]), row(sg_ln_b[li])
        w_mix = jnp.where(tril[None], w_spatial[li], 0.0)
        b_mix = jnp.broadcast_to(b_spatial[li][:, :, None], (groups, chunk, cg))
        w_mix_s = jnp.einsum("ab,gts->gatbs", jnp.eye(db, dtype=F32), w_mix[:, :ds, :ds]).reshape(groups, ts, ts)
        b_mix_s = jnp.tile(b_mix[:, :ds], (1, db, 1))
        wr_t = w_router[li].T
        br = jnp.broadcast_to(b_router[li][:, None], (n_e, 256))

        tm_p = _tile(tp, 512)
        q, k, v, u_act, vn = _proj_in(
            xp, w_in[li].astype(BF16), lng, lnb, width=d_sb, group=cg, precise=False, tm=tm_p,
            tn=_tile(d_sb, 512), dtypes=(BF16, F32, F32, BF16, BF16))
        o_sb = _attn_prompt(q, k, v, b_sb[li], batch=batch, seq=seq, heads=heads, tq=_tile(seq, 256),
                            out_dtype=BF16)
        x1p = _merge(o_sb, u_act, vn, xp, w_mix.astype(BF16), b_mix, row(g_out_sb[li]), row(g_out_cm[li]),
                     w_out[li].astype(BF16), row(ln1_g[li]), row(ln1_b[li]), precise=False, alpha=alpha,
                     heads=heads, groups=groups, tm=_tile(tp, 256))
        idx_p, gate_p = _router(x1p, wr_t.astype(BF16), br, precise=False, tm=_tile(tp, 256))

        qs, ks, vs, us, vns = _proj_in(
            xs, w_in[li], lng, lnb, width=d_sb, group=cg, precise=True, tm=ts, tn=_tile(d_sb, 512),
            dtypes=(F32,) * 5)
        q_rows = jnp.pad(qs.reshape(db, ds, heads, dh).transpose(0, 2, 1, 3),
                         ((0, 0), (0, 0), (0, SUBLANES - ds), (0, 0))).reshape(db, heads * SUBLANES, dh)
        pad_new = lambda a: jnp.pad(a.reshape(db, ds, heads, dh),
                                    ((0, 0), (0, page - ds), (0, 0), (0, 0))).reshape(db, page * heads, dh)
        o_rows = _attn_decode(q_rows, pad_new(ks), pad_new(vs),
                              cache_k[li].reshape(n_pool, page * heads, dh),
                              cache_v[li].reshape(n_pool, page * heads, dh),
                              page_table, b_sb[li], heads=heads, rows=SUBLANES, page=page)
        o_sb_s = o_rows.reshape(db, heads, SUBLANES, dh)[:, :, :ds].transpose(0, 2, 1, 3).reshape(ts, d_sb)
        x1s = _merge(o_sb_s, us, vns, xs, w_mix_s, b_mix_s, row(g_out_sb[li]), row(g_out_cm[li]), w_out[li],
                     row(ln1_g[li]), row(ln1_b[li]), precise=True, alpha=alpha, heads=heads, groups=groups,
                     tm=ts)
        idx_s, gate_s = _router(x1s, wr_t, br, precise=True, tm=ts)

        x1_all = jnp.concatenate([x1p, x1s], axis=0)
        idx_all = jnp.concatenate([idx_p, idx_s], axis=1).T
        row_tok, dest, blk_e, n_used = _dispatch_plan(idx_all, n_e, EXPERT_ROW_BLOCK)
        ys = _experts(x1_all, row_tok, blk_e, n_used, w_exp_gate, w_exp_up, w_exp_down, li,
                      blk=EXPERT_ROW_BLOCK)

        x2p = _combine(dest[:tp].T, gate_p.T, x1p, ys, w_sh_gate[li].astype(BF16), w_sh_up[li].astype(BF16),
                       w_sh_down[li].astype(BF16), row(ln2_g[li]), row(ln2_b[li]), precise=False, alpha=alpha,
                       tm=_tile(tp, 128))
        xp = _ple(x2p, p_prompt[li].reshape(tp, -1), w_ple_gate[li].astype(BF16), row(b_ple_gate[li]),
                  w_ple_proj[li].astype(BF16), row(g_ple[li]), precise=False, tm=_tile(tp, 512),
                  tn=_tile(d, 512))
        x2s = _combine(dest[tp:].T, gate_s.T, x1s, ys, w_sh_gate[li], w_sh_up[li], w_sh_down[li],
                       row(ln2_g[li]), row(ln2_b[li]), precise=True, alpha=alpha, tm=ts)
        xs = _ple(x2s, p_sample[li].reshape(ts, -1), w_ple_gate[li], row(b_ple_gate[li]), w_ple_proj[li],
                  row(g_ple[li]), precise=True, tm=ts, tn=_tile(d, 512))

        outs["kp"].append(k.reshape(batch, seq, heads, dh))
        outs["vp"].append(v.reshape(batch, seq, heads, dh))
        outs["ks"].append(ks.reshape(db, ds, heads, dh))
        outs["vs"].append(vs.reshape(db, ds, heads, dh))
        outs["cs"].append(vns.reshape(db, ds, groups, cg))

    return (xp.reshape(batch, seq, d), xs.reshape(db, ds, d),
            jnp.stack(outs["kp"], 0), jnp.stack(outs["vp"], 0), jnp.stack(outs["ks"], 0),
            jnp.stack(outs["vs"], 0), jnp.stack(outs["cs"], 0))
```

```python
import functools

import jax
import jax.numpy as jnp
from jax import lax
from jax.experimental import pallas as pl
from jax.experimental.pallas import tpu as pltpu

TOP_K = 8
N_EXPERT_GROUPS = 8
TOPK_GROUPS = 4
ROUTE_SCALE = 2.5
LN_EPS = 1e-5
RMS_EPS = 1e-6

LANES = 128
SUBLANES = 8
VMEM_LIMIT_BYTES = 56 * 1024 * 1024

EXPERT_ROW_BLOCK = 256
ATTN_HEADS_PER_STEP = 4
ATTN_ROW_CHUNK = 128
DECODE_PAGES_PER_STEP = 8
F32 = jnp.float32
BF16 = jnp.bfloat16
NEG_INF = float("-inf")


def _cparams(*sem):
    return pltpu.CompilerParams(dimension_semantics=sem, vmem_limit_bytes=VMEM_LIMIT_BYTES)


def _dot(a, b, precise):
    if precise:
        return jnp.dot(a.astype(F32), b.astype(F32), precision=lax.Precision.HIGHEST,
                       preferred_element_type=F32)
    return jnp.dot(a.astype(BF16), b.astype(BF16), preferred_element_type=F32)


def _dot_nt(a, b):
    return lax.dot_general(a, b, (((1,), (1,)), ((), ())), preferred_element_type=F32)


def _split(x):
    hi = x.astype(BF16)
    lo = (x - hi.astype(F32)).astype(BF16)
    return hi, lo


def _gelu(x):
    c = 0.7978845608028654
    return x * (0.5 * (1.0 + jnp.tanh(c * (x + 0.044715 * (x * x * x)))))


LOG2E = 1.4426950408889634


def _suffix_matrix(n):
    r = lax.broadcasted_iota(jnp.int32, (n, n), 0)
    c = lax.broadcasted_iota(jnp.int32, (n, n), 1)
    return (r > c).astype(BF16)


def _stick_terms(z2):
    drop = jnp.maximum(z2, 0.0) + jnp.log2(1.0 + jnp.exp2(-jnp.abs(z2)))
    return drop, z2 - drop


def _suffix_sums(drop, suffix):
    hi, lo = _split(drop)
    return (jnp.dot(hi, suffix, preferred_element_type=F32)
            + jnp.dot(lo, suffix, preferred_element_type=F32))


_PROJ_SEGMENT = (0, 1, 2, 3, 4, 1, 2)


def _proj_in_kernel(x_ref, w_ref, lng_ref, lnb_ref, q_ref, k_ref, v_ref, u_ref, vn_ref, kb_ref, vb_ref, *,
                    precise, n_per, group, q_scale):
    seg = pl.program_id(1) // n_per
    acc = _dot(x_ref[...], w_ref[...], precise)

    @pl.when(seg == 0)
    def _():
        q_ref[...] = (acc * q_scale).astype(q_ref.dtype)

    @pl.when(seg == 1)
    def _():
        k_ref[...] = acc.astype(k_ref.dtype)
        kb_ref[...] = acc.astype(kb_ref.dtype)

    @pl.when(seg == 2)
    def _():
        v_ref[...] = acc.astype(v_ref.dtype)
        vb_ref[...] = acc.astype(vb_ref.dtype)

    @pl.when(seg == 3)
    def _():
        u_ref[...] = _gelu(acc).astype(u_ref.dtype)

    @pl.when(seg == 4)
    def _():
        g = _gelu(acc)
        for c in range(acc.shape[1] // group):
            sl = slice(c * group, (c + 1) * group)
            xg = g[:, sl]
            mu = jnp.mean(xg, axis=-1, keepdims=True)
            xc = xg - mu
            var = jnp.mean(xc * xc, axis=-1, keepdims=True)
            y = xc * lax.rsqrt(var + LN_EPS) * lng_ref[:, sl] + lnb_ref[:, sl]
            vn_ref[:, sl] = y.astype(vn_ref.dtype)


def _proj_in(x2d, w, ln_g, ln_b, *, width, group, precise, tm, tn, dtypes, q_scale):
    t, d = x2d.shape
    n_per = width // tn
    grid = (t // tm, 5 * n_per)

    def out_map(seg):
        return lambda i, j: (i, jnp.clip(j - seg * n_per, 0, n_per - 1))

    ln_map = lambda i, j: (0, jnp.clip(j - 4 * n_per, 0, n_per - 1))
    return pl.pallas_call(
        functools.partial(_proj_in_kernel, precise=precise, n_per=n_per, group=group, q_scale=q_scale),
        grid=grid,
        in_specs=[pl.BlockSpec((tm, d), lambda i, j: (i, 0)),
                  pl.BlockSpec((d, tn), lambda i, j: (0, j)),
                  pl.BlockSpec((1, tn), ln_map),
                  pl.BlockSpec((1, tn), ln_map)],
        out_specs=[pl.BlockSpec((tm, tn), out_map(seg)) for seg in _PROJ_SEGMENT],
        out_shape=[jax.ShapeDtypeStruct((t, width), dt) for dt in dtypes],
        compiler_params=_cparams("parallel", "arbitrary"),
        name="proj_in",
    )(x2d, w, ln_g, ln_b)


def _attn_prompt_kernel(bsb_ref, q_ref, kb_ref, vb_ref, o_ref, z_ref, w_ref, *, tq, dh, hp, rc):
    hg = pl.program_id(1)
    i = pl.program_id(2)
    heads = range(hp)
    cols = [slice(h * dh, (h + 1) * dh) for h in heads]
    bias = [bsb_ref[hg * hp + h] * LOG2E for h in heads]
    q = [q_ref[:, cols[h]] for h in heads]
    suffix = _suffix_matrix(tq)
    row = lax.broadcasted_iota(jnp.int32, (tq, tq), 0)
    col = lax.broadcasted_iota(jnp.int32, (tq, tq), 1)
    causal = col < row

    def rows(kb):
        return pl.ds(pl.multiple_of(kb * tq, tq), tq)

    def scores(kb):
        return [_dot_nt(q[h], kb_ref[rows(kb), cols[h]]) + bias[h] for h in heads]

    def values(kb, accs):
        return [accs[h] + jnp.dot(w_ref[h], vb_ref[rows(kb), cols[h]], preferred_element_type=F32)
                for h in heads]

    def weights(carries, masked):
        totals = [[] for _ in heads]
        for c in range(tq // rc):
            rs = slice(c * rc, (c + 1) * rc)
            terms = [_stick_terms(z_ref[h, rs, :]) for h in heads]
            drops = [jnp.where(causal[rs], t[0], 0.0) if masked else t[0] for t in terms]
            sums = [_suffix_sums(d, suffix) for d in drops]
            for h in heads:
                w = jnp.exp2(terms[h][1] - sums[h] - carries[h][rs])
                if masked:
                    w = jnp.where(causal[rs], w, 0.0)
                w_ref[h, rs, :] = w.astype(BF16)
                totals[h].append(jnp.sum(drops[h], axis=-1, keepdims=True))
        return [carries[h] + jnp.concatenate(totals[h], axis=0) for h in heads]

    def put_scores(zs):
        for h in heads:
            z_ref[h] = zs[h]

    put_scores(scores(i))
    z_next = scores(jnp.maximum(i - 1, 0))
    carries = weights([jnp.zeros((tq, 1), F32) for _ in heads], True)
    put_scores(z_next)
    accs = [jnp.zeros((tq, dh), F32) for _ in heads]

    def step(n, state):
        accs, carries = state
        kb = i - n
        accs = values(kb + 1, accs)
        z_next = scores(jnp.maximum(kb - 1, 0))
        carries = weights(carries, False)
        put_scores(z_next)
        return accs, carries

    accs, carries = lax.fori_loop(1, i + 1, step, (accs, carries))
    accs = values(0, accs)
    for h in heads:
        o_ref[:, cols[h]] = accs[h].astype(o_ref.dtype)


def _attn_prompt(q, k, v, b_sb, *, batch, seq, heads, tq, hp, out_dtype, rc=ATTN_ROW_CHUNK):
    t, width = q.shape
    dh = width // heads
    nq = seq // tq
    grid = (batch, heads // hp, nq)
    wide = hp * dh
    return pl.pallas_call(
        functools.partial(_attn_prompt_kernel, tq=tq, dh=dh, hp=hp, rc=min(rc, tq)),
        grid_spec=pltpu.PrefetchScalarGridSpec(
            num_scalar_prefetch=0,
            grid=grid,
            in_specs=[pl.BlockSpec(memory_space=pltpu.SMEM),
                      pl.BlockSpec((tq, wide), lambda b, h, i: (b * nq + i, h)),
                      pl.BlockSpec((seq, wide), lambda b, h, i: (b, h)),
                      pl.BlockSpec((seq, wide), lambda b, h, i: (b, h))],
            out_specs=pl.BlockSpec((tq, wide), lambda b, h, i: (b * nq + i, h)),
            scratch_shapes=[pltpu.VMEM((hp, tq, tq), F32), pltpu.VMEM((hp, tq, tq), BF16)]),
        out_shape=jax.ShapeDtypeStruct((t, width), out_dtype),
        compiler_params=_cparams("arbitrary", "arbitrary", "arbitrary"),
        name="attn_prompt",
    )(b_sb, q, k, v)


def _attn_decode_kernel(pt_ref, bsb_ref, q_ref, kn_ref, vn_ref, *refs, heads, rows, page, pps):
    k_refs, v_refs = refs[:pps], refs[pps:2 * pps]
    o_ref, carry_ref = refs[2 * pps], refs[2 * pps + 1]
    p = pl.program_id(1)
    suffix = _suffix_matrix(page)
    n = heads * rows
    row = lax.broadcasted_iota(jnp.int32, (n, page), 0)
    col = lax.broadcasted_iota(jnp.int32, (n, page), 1)
    causal = col < (row % rows)
    bias = jnp.concatenate([jnp.full((rows, page), bsb_ref[h] * LOG2E, F32) for h in range(heads)], axis=0)
    q_hi, q_lo = _split(q_ref[0])
    hs = lambda x, h: x[h * rows:(h + 1) * rows]
    zero = jnp.zeros((rows, q_hi.shape[1]), BF16)
    mm = functools.partial(jnp.dot, preferred_element_type=F32)
    q_pairs = []
    for a in range(0, heads, 2):
        qa_hi, qa_lo, qb_hi, qb_lo = hs(q_hi, a), hs(q_lo, a), hs(q_hi, a + 1), hs(q_lo, a + 1)
        wide = lambda left, right: jnp.concatenate([left, right], axis=1)
        full = jnp.concatenate([wide(qa_hi, zero), wide(qa_lo, zero), wide(zero, qb_hi), wide(zero, qb_lo)], axis=0)
        top = jnp.concatenate([wide(qa_hi, zero), wide(zero, qb_hi)], axis=0)
        q_pairs.append((full, top))

    def head_pair(ref, a, axis):
        xa = _split(ref[0, pl.ds(a, page, stride=heads), :])
        xb = _split(ref[0, pl.ds(a + 1, page, stride=heads), :])
        return (jnp.concatenate([xa[0], xb[0]], axis=axis), jnp.concatenate([xa[1], xb[1]], axis=axis))

    def blocks(kv, masked):
        zs = []
        for k_ref, _ in kv:
            zh = []
            for pi, a in enumerate(range(0, heads, 2)):
                k_hi, k_lo = head_pair(k_ref, a, 1)
                r1 = _dot_nt(q_pairs[pi][0], k_hi)
                r2 = _dot_nt(q_pairs[pi][1], k_lo)
                zh.append(hs(r1, 0) + hs(r1, 1) + hs(r2, 0))
                zh.append(hs(r1, 2) + hs(r1, 3) + hs(r2, 1))
            zs.append(jnp.concatenate(zh, axis=0) + bias)
        terms = [_stick_terms(z) for z in zs]
        drops = [jnp.where(causal, t[0], 0.0) if masked else t[0] for t in terms]
        sums = [_suffix_sums(d, suffix) for d in drops]
        carry = carry_ref[...]
        for (_, v_ref), t, d, s in zip(kv, terms, drops, sums):
            w = jnp.exp2(t[1] - s - carry)
            if masked:
                w = jnp.where(causal, w, 0.0)
            carry = carry + jnp.sum(d, axis=-1, keepdims=True)
            w_hi, w_lo = _split(w)
            for a in range(0, heads, 2):
                v_hi, v_lo = head_pair(v_ref, a, 1)
                r1 = mm(jnp.concatenate([hs(w_hi, a), hs(w_lo, a), hs(w_hi, a + 1), hs(w_lo, a + 1)], axis=0), v_hi)
                r2 = mm(jnp.concatenate([hs(w_hi, a), hs(w_hi, a + 1)], axis=0), v_lo)
                dh = r1.shape[1] // 2
                o_ref[0, a * rows:(a + 1) * rows, :] += (hs(r1, 0) + hs(r1, 1) + hs(r2, 0))[:, :dh]
                o_ref[0, (a + 1) * rows:(a + 2) * rows, :] += (hs(r1, 2) + hs(r1, 3) + hs(r2, 1))[:, dh:]
        carry_ref[...] = carry

    @pl.when(p == 0)
    def _():
        o_ref[...] = jnp.zeros_like(o_ref)
        carry_ref[...] = jnp.zeros_like(carry_ref)
        blocks([(kn_ref, vn_ref)], True)

    blocks(list(zip(k_refs, v_refs)), False)


def _attn_decode(q_rows, k_new, v_new, k_pool, v_pool, page_table, b_sb, *, heads, rows, page, pps):
    db, n, dh = q_rows.shape
    n_pages = page_table.shape[1]
    blk = (1, page * heads, dh)
    seq_spec = lambda shape: pl.BlockSpec(shape, lambda b, p, pt, bs: (b, 0, 0))

    def pool_spec(j):
        return pl.BlockSpec(blk, lambda b, p, pt, bs: (pt[b, n_pages - 1 - (p * pps + j)], 0, 0))

    return pl.pallas_call(
        functools.partial(_attn_decode_kernel, heads=heads, rows=rows, page=page, pps=pps),
        grid_spec=pltpu.PrefetchScalarGridSpec(
            num_scalar_prefetch=2,
            grid=(db, n_pages // pps),
            in_specs=[seq_spec((1, n, dh)), seq_spec(blk), seq_spec(blk)]
                     + [pool_spec(j) for j in range(pps)] * 2,
            out_specs=seq_spec((1, n, dh)),
            scratch_shapes=[pltpu.VMEM((n, LANES), F32)]),
        out_shape=jax.ShapeDtypeStruct((db, n, dh), F32),
        compiler_params=_cparams("parallel", "arbitrary"),
        name="attn_decode",
    )(page_table, b_sb, q_rows, k_new, v_new, *([k_pool] * pps), *([v_pool] * pps))


def _store_slabs(ref, x):
    n, d = x.shape
    srows = d // LANES
    for s in range(srows):
        ref[pl.ds(s, n, stride=srows), :] = x[:, s * LANES:(s + 1) * LANES].astype(ref.dtype)


def _load_slabs(ref, first, count, n, srows):
    return jnp.concatenate([ref[pl.ds(first + s, n, stride=srows), :] for s in range(count)], axis=1)


def _merge_kernel(osb_ref, u_ref, vn_ref, x_ref, wm_ref, bm_ref, gsb_ref, gcm_ref, wout_ref,
                  lng_ref, lnb_ref, x1_ref, xpk_ref, ac_ref, *, precise, alpha, heads, groups, dh, cg,
                  mix_rows):
    tm = x_ref.shape[0]
    d_sb = heads * dh
    for h in range(heads):
        sl = slice(h * dh, (h + 1) * dh)
        o = osb_ref[:, sl].astype(F32)
        ms = jnp.mean(o * o, axis=-1, keepdims=True)
        ac_ref[:, sl] = (o * lax.rsqrt(ms + RMS_EPS) * gsb_ref[:, sl]).astype(ac_ref.dtype)
    for r in range(tm // mix_rows):
        rs = slice(r * mix_rows, (r + 1) * mix_rows)
        for g in range(groups):
            sl = slice(g * cg, (g + 1) * cg)
            mixed = _dot(wm_ref[g], vn_ref[rs, sl], precise) + bm_ref[g]
            ocm = u_ref[rs, sl].astype(F32) * mixed
            ms = jnp.mean(ocm * ocm, axis=-1, keepdims=True)
            ac_ref[rs, d_sb + g * cg:d_sb + (g + 1) * cg] = (
                ocm * lax.rsqrt(ms + RMS_EPS) * gcm_ref[:, sl]).astype(ac_ref.dtype)
    y = alpha * x_ref[...] + _dot(ac_ref[...], wout_ref[...], precise)
    mu = jnp.mean(y, axis=-1, keepdims=True)
    yc = y - mu
    var = jnp.mean(yc * yc, axis=-1, keepdims=True)
    x1 = yc * lax.rsqrt(var + LN_EPS) * lng_ref[...] + lnb_ref[...]
    x1_ref[...] = x1
    _store_slabs(xpk_ref, x1)


def _merge(o_sb, u_act, vn, x2d, wm, bm, g_sb, g_cm, w_out, ln_g, ln_b, *, precise, alpha, heads,
           groups, tm):
    t, d = x2d.shape
    d_sb, d_cm = o_sb.shape[1], u_act.shape[1]
    mix_rows = wm.shape[1]
    srows = d // LANES
    full = lambda a: pl.BlockSpec(a.shape, lambda i: (0,) * a.ndim)
    rows = lambda a: pl.BlockSpec((tm, a.shape[1]), lambda i: (i, 0))
    return pl.pallas_call(
        functools.partial(_merge_kernel, precise=precise, alpha=alpha, heads=heads, groups=groups,
                          dh=d_sb // heads, cg=d_cm // groups, mix_rows=mix_rows),
        grid=(t // tm,),
        in_specs=[rows(o_sb), rows(u_act), rows(vn), rows(x2d), full(wm), full(bm), full(g_sb),
                  full(g_cm), full(w_out), full(ln_g), full(ln_b)],
        out_specs=[pl.BlockSpec((tm, d), lambda i: (i, 0)),
                   pl.BlockSpec((tm * srows, LANES), lambda i: (i, 0))],
        out_shape=[jax.ShapeDtypeStruct((t, d), F32),
                   jax.ShapeDtypeStruct((t * srows, LANES), F32)],
        scratch_shapes=[pltpu.VMEM((tm, d_sb + d_cm), F32 if precise else BF16)],
        compiler_params=_cparams("parallel"),
        name="merge_out",
    )(o_sb, u_act, vn, x2d, wm, bm, g_sb, g_cm, w_out, ln_g, ln_b)


def _first_max(vals, index, size, axis):
    m = jnp.max(vals, axis=axis, keepdims=True)
    ix = jnp.min(jnp.where(vals == m, index, size), axis=axis, keepdims=True)
    return m, ix


def _lane_tile(x, n):
    return x[:, :n] if n <= LANES else jnp.concatenate([x] * (n // LANES), axis=1)


def _router_kernel(x_ref, wrt_ref, br_ref, cnt0_ref, idx_ref, gate_ref, rank_ref, cnt_ref, cnt_scr, *,
                   precise):
    n_e = wrt_ref.shape[0]
    tm = x_ref.shape[0]
    eg = n_e // N_EXPERT_GROUPS

    @pl.when(pl.program_id(0) == 0)
    def _():
        cnt_scr[...] = cnt0_ref[...]
    if precise:
        x_hi, x_lo = _split(x_ref[...])
        w_hi, w_lo = _split(wrt_ref[...])
        logits = _dot_nt(w_hi, x_hi) + _dot_nt(w_lo, x_hi) + _dot_nt(w_hi, x_lo)
    else:
        logits = _dot_nt(wrt_ref[...].astype(BF16), x_ref[...].astype(BF16))
    scores = jax.nn.sigmoid(logits)
    sel = scores + br_ref[:, :tm]
    pos = lax.broadcasted_iota(jnp.int32, (eg, tm), 0)
    sel_g, gscore = [], []
    for g in range(N_EXPERT_GROUPS):
        sg = sel[g * eg:(g + 1) * eg]
        m1, i1 = _first_max(sg, pos, eg, 0)
        m2 = jnp.max(jnp.where(pos == i1, NEG_INF, sg), axis=0, keepdims=True)
        sel_g.append(sg)
        gscore.append(m1 + m2)
    cand = []
    for g in range(N_EXPERT_GROUPS):
        beaten = jnp.zeros((1, tm), jnp.int32)
        for o in range(N_EXPERT_GROUPS):
            if o != g:
                wins = (gscore[o] >= gscore[g]) if o < g else (gscore[o] > gscore[g])
                beaten = beaten + wins.astype(jnp.int32)
        cand.append(jnp.where(beaten < TOPK_GROUPS, sel_g[g], NEG_INF))
    cand = jnp.concatenate(cand, axis=0)
    epos = lax.broadcasted_iota(jnp.int32, cand.shape, 0)
    gs, ids = [], []
    onehot = jnp.zeros(cand.shape, F32)
    for k in range(TOP_K):
        _, ei = _first_max(cand, epos, n_e, 0)
        pick = epos == ei
        idx_ref[k:k + 1, :] = ei
        ids.append(ei)
        gs.append(jnp.sum(jnp.where(pick, scores, 0.0), axis=0, keepdims=True))
        onehot = jnp.where(pick, 1.0, onehot)
        cand = jnp.where(pick, NEG_INF, cand)
    total = gs[0]
    for k in range(1, TOP_K):
        total = total + gs[k]
    for k in range(TOP_K):
        gate_ref[k:k + 1, :] = gs[k] / total * ROUTE_SCALE
    tr = lax.broadcasted_iota(jnp.int32, (tm, tm), 0)
    tc = lax.broadcasted_iota(jnp.int32, (tm, tm), 1)
    hot = onehot.astype(BF16)
    before = jnp.dot(hot, (tr < tc).astype(BF16), preferred_element_type=F32) + _lane_tile(cnt_scr[...], tm)
    for k in range(TOP_K):
        rank = jnp.sum(jnp.where(epos == ids[k], before, 0.0), axis=0, keepdims=True)
        rank_ref[k:k + 1, :] = rank.astype(jnp.int32)
    cnt_scr[...] += jnp.dot(hot, jnp.ones((tm, LANES), BF16), preferred_element_type=F32)
    cnt_ref[...] = cnt_scr[...]


def _router(x1, wr_t, br, cnt0, *, precise, tm):
    t, d = x1.shape
    n_e = wr_t.shape[0]
    whole = lambda a: pl.BlockSpec(a.shape, lambda i: (0, 0))
    per_token = pl.BlockSpec((TOP_K, tm), lambda i: (0, i))
    return pl.pallas_call(
        functools.partial(_router_kernel, precise=precise),
        grid=(t // tm,),
        in_specs=[pl.BlockSpec((tm, d), lambda i: (i, 0)), whole(wr_t), whole(br), whole(cnt0)],
        out_specs=[per_token, per_token, per_token, whole(cnt0)],
        out_shape=[jax.ShapeDtypeStruct((TOP_K, t), jnp.int32),
                   jax.ShapeDtypeStruct((TOP_K, t), F32),
                   jax.ShapeDtypeStruct((TOP_K, t), jnp.int32),
                   jax.ShapeDtypeStruct(cnt0.shape, F32)],
        scratch_shapes=[pltpu.VMEM(cnt0.shape, F32)],
        compiler_params=_cparams("arbitrary"),
        name="router",
    )(x1, wr_t, br, cnt0)


def _plan_kernel(idx_ref, rank_ref, cnt_ref, dest_ref, blk_e_ref, nused_ref, *, blk):
    n_e = cnt_ref.shape[0]
    tm = idx_ref.shape[1]
    nb = blk_e_ref.shape[1]
    cnt = cnt_ref[...]
    padded = jnp.floor((cnt + (blk - 1)) * (1.0 / blk)) * blk
    er = lax.broadcasted_iota(jnp.int32, (n_e, n_e), 0)
    ec = lax.broadcasted_iota(jnp.int32, (n_e, n_e), 1)
    pend = jnp.dot((ec <= er).astype(F32), padded, precision=lax.Precision.HIGHEST,
                   preferred_element_type=F32)
    pstart = _lane_tile(pend - padded, tm)
    epos = lax.broadcasted_iota(jnp.int32, (n_e, tm), 0)
    for k in range(TOP_K):
        base = jnp.sum(jnp.where(epos == idx_ref[k:k + 1, :], pstart, 0.0), axis=0, keepdims=True)
        dest_ref[k:k + 1, :] = rank_ref[k:k + 1, :] + base.astype(jnp.int32)
    bidx = lax.broadcasted_iota(jnp.int32, (1, nb), 1)
    ends = jnp.broadcast_to(pend[:, :1], (n_e, nb))
    owner = jnp.sum((ends <= (bidx * blk).astype(F32)).astype(jnp.int32), axis=0, keepdims=True)
    owner = jnp.minimum(owner, n_e - 1)
    n_used = (pend[n_e - 1:n_e, :1] * (1.0 / blk)).astype(jnp.int32)
    last = jnp.sum(jnp.where(bidx == n_used - 1, owner, 0), axis=1, keepdims=True)
    blk_e_ref[...] = jnp.where(bidx < n_used, owner, last)
    nused_ref[...] = jnp.broadcast_to(n_used, nused_ref.shape)


def _plan(idx, rank, cnt, *, blk, n_blocks, tm):
    k, t = idx.shape
    per_token = pl.BlockSpec((k, tm), lambda i: (0, i))
    fixed = lambda shape: pl.BlockSpec(shape, lambda i: (0, 0))
    return pl.pallas_call(
        functools.partial(_plan_kernel, blk=blk),
        grid=(t // tm,),
        in_specs=[per_token, per_token, fixed(cnt.shape)],
        out_specs=[per_token, fixed((1, n_blocks)), fixed((1, LANES))],
        out_shape=[jax.ShapeDtypeStruct((k, t), jnp.int32),
                   jax.ShapeDtypeStruct((1, n_blocks), jnp.int32),
                   jax.ShapeDtypeStruct((1, LANES), jnp.int32)],
        compiler_params=_cparams("arbitrary"),
        name="dispatch_plan",
    )(idx, rank, cnt)


def _scatter_kernel(dest_ref, x_ref, *rest, srows):
    xs_hbm, sem = rest[-2], rest[-1]
    n_tok = x_ref.shape[0] // srows

    def body(r, carry):
        src = x_ref.at[pl.ds(r * srows, srows)]
        for k in range(TOP_K):
            pltpu.make_async_copy(src, xs_hbm.at[pl.ds(dest_ref[0, k, r] * srows, srows)], sem).start(priority=k % 2)
        return carry

    lax.fori_loop(0, n_tok, body, 0)
    for k in range(TOP_K):
        pltpu.make_async_copy(x_ref, xs_hbm.at[pl.ds(0, x_ref.shape[0])], sem).wait()


def _scatter_rows(dest3, xpk, xs_prev, *, n_rows, srows, tm):
    steps = dest3.shape[0]
    in_specs = [pl.BlockSpec((1, TOP_K, tm), lambda i: (i, 0, 0), memory_space=pltpu.SMEM),
                pl.BlockSpec((tm * srows, LANES), lambda i: (i, 0))]
    args = [dest3, xpk]
    aliases = {}
    if xs_prev is not None:
        in_specs.append(pl.BlockSpec(memory_space=pl.ANY))
        args.append(xs_prev)
        aliases = {2: 0}
    return pl.pallas_call(
        functools.partial(_scatter_kernel, srows=srows),
        grid=(steps,),
        in_specs=in_specs,
        out_specs=pl.BlockSpec(memory_space=pl.ANY),
        out_shape=jax.ShapeDtypeStruct((n_rows * srows, LANES), xpk.dtype),
        scratch_shapes=[pltpu.SemaphoreType.DMA(())],
        input_output_aliases=aliases,
        compiler_params=_cparams("arbitrary"),
        name="dispatch_rows",
    )(*args)


def _expert_kernel(blk_e_ref, nused_ref, xs_ref, wg_hbm, wu_hbm, wd_hbm, ys_ref,
                   wg_buf, wu_buf, wd_buf, sem, slot_ref, *, layer, srows):
    s = pl.program_id(0)
    n_used = nused_ref[0]
    blk = xs_ref.shape[0] // srows
    kc = 2 * LANES

    def weight_copies(e, slot):
        return (pltpu.make_async_copy(wg_hbm.at[layer, e], wg_buf.at[slot], sem.at[slot, 0]),
                pltpu.make_async_copy(wu_hbm.at[layer, e], wu_buf.at[slot], sem.at[slot, 1]),
                pltpu.make_async_copy(wd_hbm.at[layer, e], wd_buf.at[slot], sem.at[slot, 2]))

    @pl.when(s < n_used)
    def _():
        e = blk_e_ref[s]
        first = jnp.logical_or(s == 0, e != blk_e_ref[jnp.maximum(s - 1, 0)])

        @pl.when(s == 0)
        def _():
            slot_ref[0] = 0
            for c in weight_copies(e, 0):
                c.start()

        @pl.when(jnp.logical_and(first, s > 0))
        def _():
            slot_ref[0] = 1 - slot_ref[0]

        slot = slot_ref[0]

        @pl.when(first)
        def _():
            for c in weight_copies(e, slot):
                c.wait()
            nxt = lax.while_loop(lambda j: jnp.logical_and(j < n_used, blk_e_ref[jnp.minimum(j, n_used - 1)] == e),
                                 lambda j: j + 1, s + 1)

            @pl.when(nxt < n_used)
            def _():
                for c in weight_copies(blk_e_ref[jnp.minimum(nxt, n_used - 1)], 1 - slot):
                    c.start()

        wg, wu, wd = wg_buf.at[slot], wu_buf.at[slot], wd_buf.at[slot]
        gate = up = None
        for r in range(srows // 2):
            x = _load_slabs(xs_ref, 2 * r, 2, blk, srows).astype(BF16)
            g = jnp.dot(x, wg[r * kc:(r + 1) * kc, :].astype(BF16), preferred_element_type=F32)
            u = jnp.dot(x, wu[r * kc:(r + 1) * kc, :].astype(BF16), preferred_element_type=F32)
            gate = g if gate is None else gate + g
            up = u if up is None else up + u
        hid = ((gate * jax.nn.sigmoid(gate)) * up).astype(BF16)
        _store_slabs(ys_ref, jnp.dot(hid, wd[...].astype(BF16), preferred_element_type=F32))


def _experts(xs, blk_e, n_used, w_eg, w_eu, w_ed, layer, *, blk, srows):
    n_blocks = blk_e.shape[0]
    d, de = w_eg.shape[2], w_eg.shape[3]
    rows = lambda s, be, nu: (jnp.minimum(s, nu[0] - 1), 0)
    hbm = pl.BlockSpec(memory_space=pl.ANY)
    return pl.pallas_call(
        functools.partial(_expert_kernel, layer=layer, srows=srows),
        grid_spec=pltpu.PrefetchScalarGridSpec(
            num_scalar_prefetch=2,
            grid=(n_blocks,),
            in_specs=[pl.BlockSpec((blk * srows, LANES), rows), hbm, hbm, hbm],
            out_specs=pl.BlockSpec((blk * srows, LANES), rows),
            scratch_shapes=[pltpu.VMEM((2, d, de), F32), pltpu.VMEM((2, d, de), F32),
                            pltpu.VMEM((2, de, d), F32), pltpu.SemaphoreType.DMA((2, 3)),
                            pltpu.SMEM((1,), jnp.int32)]),
        out_shape=jax.ShapeDtypeStruct(xs.shape, xs.dtype),
        compiler_params=_cparams("arbitrary"),
        name="experts",
    )(blk_e, n_used, xs, w_eg, w_eu, w_ed)


def _combine_kernel(dest_ref, dest_next_ref, gate_ref, x1_ref, ys_hbm, wsg_ref, wsu_ref, wsd_ref,
                    lng_ref, lnb_ref, x2_ref, ybuf, sem, *, precise, alpha, srows):
    i = pl.program_id(0)
    n_steps = pl.num_programs(0)
    tm = x1_ref.shape[0]
    slot = i % 2

    def issue(ids_ref, to_slot):
        def body(r, carry):
            for k in range(TOP_K):
                pltpu.make_async_copy(ys_hbm.at[pl.ds(ids_ref[0, k, r] * srows, srows)],
                                      ybuf.at[to_slot, k, pl.ds(r * srows, srows)],
                                      sem.at[to_slot]).start(priority=k % 2)
            return carry
        lax.fori_loop(0, tm, body, 0)

    @pl.when(i == 0)
    def _():
        issue(dest_ref, 0)

    @pl.when(i + 1 < n_steps)
    def _():
        issue(dest_next_ref, 1 - slot)

    x1 = x1_ref[...]
    hid = jax.nn.silu(_dot(x1, wsg_ref[...], precise)) * _dot(x1, wsu_ref[...], precise)
    y = alpha * x1 + _dot(hid, wsd_ref[...], precise)
    for k in range(TOP_K):
        pltpu.make_async_copy(ys_hbm.at[pl.ds(0, tm * srows)], ybuf.at[slot, k], sem.at[slot]).wait()
    gates = [jnp.broadcast_to(gate_ref[:, k:k + 1], (tm, LANES)) for k in range(TOP_K)]
    pieces = []
    for r in range(srows):
        acc = gates[0] * ybuf[slot, 0, pl.ds(r, tm, stride=srows), :]
        for k in range(1, TOP_K):
            acc = acc + gates[k] * ybuf[slot, k, pl.ds(r, tm, stride=srows), :]
        pieces.append(acc)
    y = y + jnp.concatenate(pieces, axis=1)
    mu = jnp.mean(y, axis=-1, keepdims=True)
    yc = y - mu
    var = jnp.mean(yc * yc, axis=-1, keepdims=True)
    x2_ref[...] = yc * lax.rsqrt(var + LN_EPS) * lng_ref[...] + lnb_ref[...]


def _combine(dest3, gates, x1, ys, wsg, wsu, wsd, ln_g, ln_b, *, precise, alpha, srows):
    t, d = x1.shape
    n_steps, _, tm = dest3.shape
    full = lambda a: pl.BlockSpec(a.shape, lambda i: (0,) * a.ndim)
    return pl.pallas_call(
        functools.partial(_combine_kernel, precise=precise, alpha=alpha, srows=srows),
        grid=(n_steps,),
        in_specs=[pl.BlockSpec((1, TOP_K, tm), lambda i: (i, 0, 0), memory_space=pltpu.SMEM),
                  pl.BlockSpec((1, TOP_K, tm), lambda i: (jnp.minimum(i + 1, n_steps - 1), 0, 0),
                               memory_space=pltpu.SMEM),
                  pl.BlockSpec((tm, TOP_K), lambda i: (i, 0)),
                  pl.BlockSpec((tm, d), lambda i: (i, 0)),
                  pl.BlockSpec(memory_space=pl.ANY),
                  full(wsg), full(wsu), full(wsd), full(ln_g), full(ln_b)],
        out_specs=pl.BlockSpec((tm, d), lambda i: (i, 0)),
        out_shape=jax.ShapeDtypeStruct((t, d), F32),
        scratch_shapes=[pltpu.VMEM((2, TOP_K, tm * srows, LANES), ys.dtype), pltpu.SemaphoreType.DMA((2,))],
        compiler_params=_cparams("arbitrary"),
        name="combine",
    )(dest3, dest3, gates, x1, ys, wsg, wsu, wsd, ln_g, ln_b)


def _ple_kernel(x_ref, xcol_ref, p_ref, wg_ref, bg_ref, wp_ref, gp_ref, y_ref, e_ref, *, precise):
    j = pl.program_id(1)
    tn = y_ref.shape[1]

    @pl.when(j == 0)
    def _():
        e = _dot(p_ref[...], wp_ref[...], precise)
        ms = jnp.mean(e * e, axis=-1, keepdims=True)
        e = e * lax.rsqrt(ms + RMS_EPS) * gp_ref[...]
        for jj in range(e_ref.shape[0]):
            e_ref[jj] = e[:, jj * tn:(jj + 1) * tn]

    gate = jax.nn.sigmoid(_dot(x_ref[...], wg_ref[...], precise) + bg_ref[...])
    y_ref[...] = xcol_ref[...] + gate * e_ref[j]


def _ple(x2, p2d, w_gate, b_gate, w_proj, g_proj, *, precise, tm, tn):
    t, d = x2.shape
    dp = p2d.shape[1]
    return pl.pallas_call(
        functools.partial(_ple_kernel, precise=precise),
        grid=(t // tm, d // tn),
        in_specs=[pl.BlockSpec((tm, d), lambda i, j: (i, 0)),
                  pl.BlockSpec((tm, tn), lambda i, j: (i, j)),
                  pl.BlockSpec((tm, dp), lambda i, j: (i, 0)),
                  pl.BlockSpec((d, tn), lambda i, j: (0, j)),
                  pl.BlockSpec((1, tn), lambda i, j: (0, j)),
                  pl.BlockSpec((dp, d), lambda i, j: (0, 0)),
                  pl.BlockSpec((1, d), lambda i, j: (0, 0))],
        out_specs=pl.BlockSpec((tm, tn), lambda i, j: (i, j)),
        out_shape=jax.ShapeDtypeStruct((t, d), F32),
        scratch_shapes=[pltpu.VMEM((d // tn, tm, tn), F32)],
        compiler_params=_cparams("parallel", "arbitrary"),
        name="ple",
    )(x2, x2, p2d, w_gate, b_gate, w_proj, g_proj)


def _tile(n, want):
    t = min(n, want)
    while n % t:
        t -= 1
    return t


def kernel(x_prompt, x_sample, p_prompt, p_sample, cache_k, cache_v, page_table, w_in, b_sb, sg_ln_g, sg_ln_b, w_spatial, b_spatial, g_out_sb, g_out_cm, w_out, ln1_g, ln1_b, w_router, b_router, w_exp_gate, w_exp_up, w_exp_down, w_sh_gate, w_sh_up, w_sh_down, ln2_g, ln2_b, w_ple_gate, b_ple_gate, w_ple_proj, g_ple):
    batch, seq, d = x_prompt.shape
    db, ds, _ = x_sample.shape
    depth = w_in.shape[0]
    n_pool, page, heads, dh = cache_k.shape[1:]
    d_sb = heads * dh
    d_cm = sg_ln_g.shape[1]
    groups, chunk = w_spatial.shape[1], w_spatial.shape[2]
    cg = d_cm // groups
    n_e = w_router.shape[2]
    alpha = float((2 * depth) ** 0.25)
    tp, ts = batch * seq, db * ds
    assert d_sb == d_cm and w_in.shape[2] == 3 * d_sb + 2 * d_cm
    assert seq % chunk == 0 and ds <= min(chunk, SUBLANES) and page == chunk == LANES
    assert d % (2 * LANES) == 0
    srows = d // LANES

    row = lambda a: a.reshape(1, -1)
    tril = jnp.tril(jnp.ones((chunk, chunk), jnp.bool_))
    xp = x_prompt.reshape(tp, d)
    xs = x_sample.reshape(ts, d)
    outs = {k: [] for k in ("kp", "vp", "ks", "vs", "cs")}

    for li in range(depth):
        lng, lnb = row(sg_ln_g[li]), row(sg_ln_b[li])
        w_mix = jnp.where(tril[None], w_spatial[li], 0.0)
        b_mix = jnp.broadcast_to(b_spatial[li][:, :, None], (groups, chunk, cg))
        w_mix_s = jnp.einsum("ab,gts->gatbs", jnp.eye(db, dtype=F32), w_mix[:, :ds, :ds]).reshape(groups, ts, ts)
        b_mix_s = jnp.tile(b_mix[:, :ds], (1, db, 1))
        wr_t = w_router[li].T
        br = jnp.broadcast_to(b_router[li][:, None], (n_e, 256))

        q_scale = LOG2E * dh ** -0.5

        q, k, v, u_act, vn, k_b, v_b = _proj_in(
            xp, w_in[li].astype(BF16), lng, lnb, width=d_sb, group=cg, precise=False, tm=_tile(tp, 512),
            tn=_tile(d_sb, 512), dtypes=(BF16, F32, F32, BF16, BF16, BF16, BF16), q_scale=q_scale)
        o_sb = _attn_prompt(q, k_b, v_b, b_sb[li], batch=batch, seq=seq, heads=heads, tq=_tile(seq, 256),
                            hp=_tile(heads, ATTN_HEADS_PER_STEP), out_dtype=BF16)
        x1p, xpk_p = _merge(o_sb, u_act, vn, xp, w_mix.astype(BF16), b_mix, row(g_out_sb[li]),
                            row(g_out_cm[li]), w_out[li].astype(BF16), row(ln1_g[li]), row(ln1_b[li]),
                            precise=False, alpha=alpha, heads=heads, groups=groups, tm=_tile(tp, 256))
        cnt0 = jnp.zeros((n_e, LANES), F32)
        idx_p, gate_p, rank_p, cnt_p = _router(x1p, wr_t.astype(BF16), br, cnt0, precise=False,
                                               tm=_tile(tp, 256))

        qs, ks, vs, us, vns, _, _ = _proj_in(
            xs, w_in[li], lng, lnb, width=d_sb, group=cg, precise=True, tm=ts, tn=_tile(d_sb, 512),
            dtypes=(F32,) * 7, q_scale=q_scale)
        q_rows = jnp.pad(qs.reshape(db, ds, heads, dh).transpose(0, 2, 1, 3),
                         ((0, 0), (0, 0), (0, SUBLANES - ds), (0, 0))).reshape(db, heads * SUBLANES, dh)
        pad_new = lambda a: jnp.pad(a.reshape(db, ds, heads, dh),
                                    ((0, 0), (0, page - ds), (0, 0), (0, 0))).reshape(db, page * heads, dh)
        o_rows = _attn_decode(q_rows, pad_new(ks), pad_new(vs),
                              cache_k[li].reshape(n_pool, page * heads, dh),
                              cache_v[li].reshape(n_pool, page * heads, dh),
                              page_table, b_sb[li], heads=heads, rows=SUBLANES, page=page,
                              pps=_tile(page_table.shape[1], DECODE_PAGES_PER_STEP))
        o_sb_s = o_rows.reshape(db, heads, SUBLANES, dh)[:, :, :ds].transpose(0, 2, 1, 3).reshape(ts, d_sb)
        x1s, xpk_s = _merge(o_sb_s, us, vns, xs, w_mix_s, b_mix_s, row(g_out_sb[li]), row(g_out_cm[li]),
                            w_out[li], row(ln1_g[li]), row(ln1_b[li]), precise=True, alpha=alpha, heads=heads,
                            groups=groups, tm=ts)
        idx_s, gate_s, rank_s, cnt = _router(x1s, wr_t, br, cnt_p, precise=True, tm=ts)

        blk = EXPERT_ROW_BLOCK
        n_rows = ((tp + ts) * TOP_K + n_e * (blk - 1) + blk - 1) // blk * blk
        tm_c = _tile(tp, 128)
        steps = lambda dest, tm: dest.reshape(TOP_K, -1, tm).transpose(1, 0, 2)
        dest_p, _, _ = _plan(idx_p, rank_p, cnt, blk=blk, n_blocks=n_rows // blk, tm=_tile(tp, 256))
        dest_s, blk_e, n_used = _plan(idx_s, rank_s, cnt, blk=blk, n_blocks=n_rows // blk, tm=ts)
        dest_p, dest_s = steps(dest_p, tm_c), steps(dest_s, ts)
        rows_x = _scatter_rows(dest_p, xpk_p, None, n_rows=n_rows, srows=srows, tm=tm_c)
        rows_x = _scatter_rows(dest_s, xpk_s, rows_x, n_rows=n_rows, srows=srows, tm=ts)
        rows_y = _experts(rows_x, blk_e[0], n_used[0, :1], w_exp_gate, w_exp_up, w_exp_down, li,
                          blk=blk, srows=srows)

        x2p = _combine(dest_p, gate_p.T, x1p, rows_y, w_sh_gate[li].astype(BF16), w_sh_up[li].astype(BF16),
                       w_sh_down[li].astype(BF16), row(ln2_g[li]), row(ln2_b[li]), precise=False, alpha=alpha,
                       srows=srows)
        xp = _ple(x2p, p_prompt[li].reshape(tp, -1), w_ple_gate[li].astype(BF16), row(b_ple_gate[li]),
                  w_ple_proj[li].astype(BF16), row(g_ple[li]), precise=False, tm=_tile(tp, 512),
                  tn=_tile(d, 512))
        x2s = _combine(dest_s, gate_s.T, x1s, rows_y, w_sh_gate[li], w_sh_up[li], w_sh_down[li],
                       row(ln2_g[li]), row(ln2_b[li]), precise=True, alpha=alpha, srows=srows)
        xs = _ple(x2s, p_sample[li].reshape(ts, -1), w_ple_gate[li], row(b_ple_gate[li]), w_ple_proj[li],
                  row(g_ple[li]), precise=True, tm=ts, tn=_tile(d, 512))

        outs["kp"].append(k.reshape(batch, seq, heads, dh))
        outs["vp"].append(v.reshape(batch, seq, heads, dh))
        outs["ks"].append(ks.reshape(db, ds, heads, dh))
        outs["vs"].append(vs.reshape(db, ds, heads, dh))
        outs["cs"].append(vns.reshape(db, ds, groups, cg))

    return (xp.reshape(batch, seq, d), xs.reshape(db, ds, d),
            jnp.stack(outs["kp"], 0), jnp.stack(outs["vp"], 0), jnp.stack(outs["ks"], 0),
            jnp.stack(outs["vs"], 0), jnp.stack(outs["cs"], 0))
```

```python
import functools

import jax
import jax.numpy as jnp
from jax import lax
from jax.experimental import pallas as pl
from jax.experimental.pallas import tpu as pltpu

TOP_K = 8
N_EXPERT_GROUPS = 8
TOPK_GROUPS = 4
ROUTE_SCALE = 2.5
LN_EPS = 1e-5
RMS_EPS = 1e-6

LANES = 128
SUBLANES = 8
VMEM_LIMIT_BYTES = 56 * 1024 * 1024

EXPERT_ROW_BLOCK = 256
ATTN_HEADS_PER_STEP = 4
ATTN_ROW_CHUNK = 128
DECODE_PAGES_PER_STEP = 8
F32 = jnp.float32
BF16 = jnp.bfloat16
NEG_INF = float("-inf")


def _cparams(*sem):
    return pltpu.CompilerParams(dimension_semantics=sem, vmem_limit_bytes=VMEM_LIMIT_BYTES)


def _dot(a, b, precise):
    if precise:
        return jnp.dot(a.astype(F32), b.astype(F32), precision=lax.Precision.HIGHEST,
                       preferred_element_type=F32)
    return jnp.dot(a.astype(BF16), b.astype(BF16), preferred_element_type=F32)


def _dot_nt(a, b):
    return lax.dot_general(a, b, (((1,), (1,)), ((), ())), preferred_element_type=F32)


def _split(x):
    hi = x.astype(BF16)
    lo = (x - hi.astype(F32)).astype(BF16)
    return hi, lo


def _gelu(x):
    c = 0.7978845608028654
    return x * (0.5 * (1.0 + jnp.tanh(c * (x + 0.044715 * (x * x * x)))))


LOG2E = 1.4426950408889634


def _suffix_matrix(n):
    r = lax.broadcasted_iota(jnp.int32, (n, n), 0)
    c = lax.broadcasted_iota(jnp.int32, (n, n), 1)
    return (r > c).astype(BF16)


def _stick_terms(z2):
    drop = jnp.maximum(z2, 0.0) + jnp.log2(1.0 + jnp.exp2(-jnp.abs(z2)))
    return drop, z2 - drop


def _suffix_sums(drop, suffix):
    hi, lo = _split(drop)
    return (jnp.dot(hi, suffix, preferred_element_type=F32)
            + jnp.dot(lo, suffix, preferred_element_type=F32))


_PROJ_SEGMENT = (0, 1, 2, 3, 4, 1, 2)


def _proj_in_kernel(x_ref, w_ref, lng_ref, lnb_ref, q_ref, k_ref, v_ref, u_ref, vn_ref, kb_ref, vb_ref, *,
                    precise, n_per, group, q_scale):
    seg = pl.program_id(1) // n_per
    acc = _dot(x_ref[...], w_ref[...], precise)

    @pl.when(seg == 0)
    def _():
        q_ref[...] = (acc * q_scale).astype(q_ref.dtype)

    @pl.when(seg == 1)
    def _():
        k_ref[...] = acc.astype(k_ref.dtype)
        kb_ref[...] = acc.astype(kb_ref.dtype)

    @pl.when(seg == 2)
    def _():
        v_ref[...] = acc.astype(v_ref.dtype)
        vb_ref[...] = acc.astype(vb_ref.dtype)

    @pl.when(seg == 3)
    def _():
        u_ref[...] = _gelu(acc).astype(u_ref.dtype)

    @pl.when(seg == 4)
    def _():
        g = _gelu(acc)
        for c in range(acc.shape[1] // group):
            sl = slice(c * group, (c + 1) * group)
            xg = g[:, sl]
            mu = jnp.mean(xg, axis=-1, keepdims=True)
            xc = xg - mu
            var = jnp.mean(xc * xc, axis=-1, keepdims=True)
            y = xc * lax.rsqrt(var + LN_EPS) * lng_ref[:, sl] + lnb_ref[:, sl]
            vn_ref[:, sl] = y.astype(vn_ref.dtype)


def _proj_in(x2d, w, ln_g, ln_b, *, width, group, precise, tm, tn, dtypes, q_scale):
    t, d = x2d.shape
    n_per = width // tn
    grid = (t // tm, 5 * n_per)

    def out_map(seg):
        return lambda i, j: (i, jnp.clip(j - seg * n_per, 0, n_per - 1))

    ln_map = lambda i, j: (0, jnp.clip(j - 4 * n_per, 0, n_per - 1))
    return pl.pallas_call(
        functools.partial(_proj_in_kernel, precise=precise, n_per=n_per, group=group, q_scale=q_scale),
        grid=grid,
        in_specs=[pl.BlockSpec((tm, d), lambda i, j: (i, 0)),
                  pl.BlockSpec((d, tn), lambda i, j: (0, j)),
                  pl.BlockSpec((1, tn), ln_map),
                  pl.BlockSpec((1, tn), ln_map)],
        out_specs=[pl.BlockSpec((tm, tn), out_map(seg)) for seg in _PROJ_SEGMENT],
        out_shape=[jax.ShapeDtypeStruct((t, width), dt) for dt in dtypes],
        compiler_params=_cparams("parallel", "arbitrary"),
        name="proj_in",
    )(x2d, w, ln_g, ln_b)


def _attn_prompt_kernel(bsb_ref, q_ref, kb_ref, vb_ref, o_ref, z_ref, w_ref, *, tq, dh, hp, rc):
    hg = pl.program_id(1)
    i = pl.program_id(2)
    heads = range(hp)
    cols = [slice(h * dh, (h + 1) * dh) for h in heads]
    bias = [bsb_ref[hg * hp + h] * LOG2E for h in heads]
    q = [q_ref[:, cols[h]] for h in heads]
    suffix = _suffix_matrix(tq)
    row = lax.broadcasted_iota(jnp.int32, (tq, tq), 0)
    col = lax.broadcasted_iota(jnp.int32, (tq, tq), 1)
    causal = col < row

    def rows(kb):
        return pl.ds(pl.multiple_of(kb * tq, tq), tq)

    def scores(kb):
        return [_dot_nt(q[h], kb_ref[rows(kb), cols[h]]) + bias[h] for h in heads]

    def values(kb, accs):
        return [accs[h] + jnp.dot(w_ref[h], vb_ref[rows(kb), cols[h]], preferred_element_type=F32)
                for h in heads]

    def weights(carries, masked):
        totals = [[] for _ in heads]
        for c in range(tq // rc):
            rs = slice(c * rc, (c + 1) * rc)
            terms = [_stick_terms(z_ref[h, rs, :]) for h in heads]
            drops = [jnp.where(causal[rs], t[0], 0.0) if masked else t[0] for t in terms]
            sums = [_suffix_sums(d, suffix) for d in drops]
            for h in heads:
                w = jnp.exp2(terms[h][1] - sums[h] - carries[h][rs])
                if masked:
                    w = jnp.where(causal[rs], w, 0.0)
                w_ref[h, rs, :] = w.astype(BF16)
                totals[h].append(jnp.sum(drops[h], axis=-1, keepdims=True))
        return [carries[h] + jnp.concatenate(totals[h], axis=0) for h in heads]

    def put_scores(zs):
        for h in heads:
            z_ref[h] = zs[h]

    put_scores(scores(i))
    z_next = scores(jnp.maximum(i - 1, 0))
    carries = weights([jnp.zeros((tq, 1), F32) for _ in heads], True)
    put_scores(z_next)
    accs = [jnp.zeros((tq, dh), F32) for _ in heads]

    def step(n, state):
        accs, carries = state
        kb = i - n
        accs = values(kb + 1, accs)
        z_next = scores(jnp.maximum(kb - 1, 0))
        carries = weights(carries, False)
        put_scores(z_next)
        return accs, carries

    accs, carries = lax.fori_loop(1, i + 1, step, (accs, carries))
    accs = values(0, accs)
    for h in heads:
        o_ref[:, cols[h]] = accs[h].astype(o_ref.dtype)


def _attn_prompt(q, k, v, b_sb, *, batch, seq, heads, tq, hp, out_dtype, rc=ATTN_ROW_CHUNK):
    t, width = q.shape
    dh = width // heads
    nq = seq // tq
    grid = (batch, heads // hp, nq)
    wide = hp * dh
    return pl.pallas_call(
        functools.partial(_attn_prompt_kernel, tq=tq, dh=dh, hp=hp, rc=min(rc, tq)),
        grid_spec=pltpu.PrefetchScalarGridSpec(
            num_scalar_prefetch=0,
            grid=grid,
            in_specs=[pl.BlockSpec(memory_space=pltpu.SMEM),
                      pl.BlockSpec((tq, wide), lambda b, h, i: (b * nq + i, h)),
                      pl.BlockSpec((seq, wide), lambda b, h, i: (b, h)),
                      pl.BlockSpec((seq, wide), lambda b, h, i: (b, h))],
            out_specs=pl.BlockSpec((tq, wide), lambda b, h, i: (b * nq + i, h)),
            scratch_shapes=[pltpu.VMEM((hp, tq, tq), F32), pltpu.VMEM((hp, tq, tq), BF16)]),
        out_shape=jax.ShapeDtypeStruct((t, width), out_dtype),
        compiler_params=_cparams("arbitrary", "arbitrary", "arbitrary"),
        name="attn_prompt",
    )(b_sb, q, k, v)


def _attn_decode_kernel(pt_ref, bsb_ref, q_ref, kn_ref, vn_ref, *refs, heads, rows, page, pps):
    k_refs, v_refs = refs[:pps], refs[pps:2 * pps]
    o_ref, carry_ref = refs[2 * pps], refs[2 * pps + 1]
    p = pl.program_id(1)
    suffix = _suffix_matrix(page)
    n = heads * rows
    row = lax.broadcasted_iota(jnp.int32, (n, page), 0)
    col = lax.broadcasted_iota(jnp.int32, (n, page), 1)
    causal = col < (row % rows)
    bias = jnp.concatenate([jnp.full((rows, page), bsb_ref[h] * LOG2E, F32) for h in range(heads)], axis=0)
    q_hi, q_lo = _split(q_ref[0])
    hs = lambda x, h: x[h * rows:(h + 1) * rows]
    zero = jnp.zeros((rows, q_hi.shape[1]), BF16)
    mm = functools.partial(jnp.dot, preferred_element_type=F32)
    q_pairs = []
    for a in range(0, heads, 2):
        qa_hi, qa_lo, qb_hi, qb_lo = hs(q_hi, a), hs(q_lo, a), hs(q_hi, a + 1), hs(q_lo, a + 1)
        wide = lambda left, right: jnp.concatenate([left, right], axis=1)
        full = jnp.concatenate([wide(qa_hi, zero), wide(qa_lo, zero), wide(zero, qb_hi), wide(zero, qb_lo)], axis=0)
        top = jnp.concatenate([wide(qa_hi, zero), wide(zero, qb_hi)], axis=0)
        q_pairs.append((full, top))

    def head_pair(ref, a, axis):
        xa = _split(ref[0, pl.ds(a, page, stride=heads), :])
        xb = _split(ref[0, pl.ds(a + 1, page, stride=heads), :])
        return (jnp.concatenate([xa[0], xb[0]], axis=axis), jnp.concatenate([xa[1], xb[1]], axis=axis))

    def blocks(kv, masked):
        zs = []
        for k_ref, _ in kv:
            zh = []
            for pi, a in enumerate(range(0, heads, 2)):
                k_hi, k_lo = head_pair(k_ref, a, 1)
                r1 = _dot_nt(q_pairs[pi][0], k_hi)
                r2 = _dot_nt(q_pairs[pi][1], k_lo)
                zh.append(hs(r1, 0) + hs(r1, 1) + hs(r2, 0))
                zh.append(hs(r1, 2) + hs(r1, 3) + hs(r2, 1))
            zs.append(jnp.concatenate(zh, axis=0) + bias)
        terms = [_stick_terms(z) for z in zs]
        drops = [jnp.where(causal, t[0], 0.0) if masked else t[0] for t in terms]
        sums = [_suffix_sums(d, suffix) for d in drops]
        carry = carry_ref[...]
        for (_, v_ref), t, d, s in zip(kv, terms, drops, sums):
            w = jnp.exp2(t[1] - s - carry)
            if masked:
                w = jnp.where(causal, w, 0.0)
            carry = carry + jnp.sum(d, axis=-1, keepdims=True)
            w_hi, w_lo = _split(w)
            for a in range(0, heads, 2):
                v_hi, v_lo = head_pair(v_ref, a, 1)
                r1 = mm(jnp.concatenate([hs(w_hi, a), hs(w_lo, a), hs(w_hi, a + 1), hs(w_lo, a + 1)], axis=0), v_hi)
                r2 = mm(jnp.concatenate([hs(w_hi, a), hs(w_hi, a + 1)], axis=0), v_lo)
                dh = r1.shape[1] // 2
                o_ref[0, a * rows:(a + 1) * rows, :] += (hs(r1, 0) + hs(r1, 1) + hs(r2, 0))[:, :dh]
                o_ref[0, (a + 1) * rows:(a + 2) * rows, :] += (hs(r1, 2) + hs(r1, 3) + hs(r2, 1))[:, dh:]
        carry_ref[...] = carry

    @pl.when(p == 0)
    def _():
        o_ref[...] = jnp.zeros_like(o_ref)
        carry_ref[...] = jnp.zeros_like(carry_ref)
        blocks([(kn_ref, vn_ref)], True)

    blocks(list(zip(k_refs, v_refs)), False)


def _attn_decode(q_rows, k_new, v_new, k_pool, v_pool, page_table, b_sb, *, heads, rows, page, pps):
    db, n, dh = q_rows.shape
    n_pages = page_table.shape[1]
    blk = (1, page * heads, dh)
    seq_spec = lambda shape: pl.BlockSpec(shape, lambda b, p, pt, bs: (b, 0, 0))

    def pool_spec(j):
        return pl.BlockSpec(blk, lambda b, p, pt, bs: (pt[b, n_pages - 1 - (p * pps + j)], 0, 0))

    return pl.pallas_call(
        functools.partial(_attn_decode_kernel, heads=heads, rows=rows, page=page, pps=pps),
        grid_spec=pltpu.PrefetchScalarGridSpec(
            num_scalar_prefetch=2,
            grid=(db, n_pages // pps),
            in_specs=[seq_spec((1, n, dh)), seq_spec(blk), seq_spec(blk)]
                     + [pool_spec(j) for j in range(pps)] * 2,
            out_specs=seq_spec((1, n, dh)),
            scratch_shapes=[pltpu.VMEM((n, LANES), F32)]),
        out_shape=jax.ShapeDtypeStruct((db, n, dh), F32),
        compiler_params=_cparams("parallel", "arbitrary"),
        name="attn_decode",
    )(page_table, b_sb, q_rows, k_new, v_new, *([k_pool] * pps), *([v_pool] * pps))


def _to_slabs(x, dtype):
    n, d = x.shape
    return x.astype(dtype).reshape(n * (d // LANES), LANES)


def _from_slabs(s, srows):
    return s.reshape(s.shape[0] // srows, srows * LANES)


def _merge_kernel(osb_ref, u_ref, vn_ref, x_ref, wm_ref, bm_ref, gsb_ref, gcm_ref, wout_ref,
                  lng_ref, lnb_ref, x1_ref, xpk_ref, ac_ref, *, precise, alpha, heads, groups, dh, cg,
                  mix_rows):
    tm = x_ref.shape[0]
    d_sb = heads * dh
    for h in range(heads):
        sl = slice(h * dh, (h + 1) * dh)
        o = osb_ref[:, sl].astype(F32)
        ms = jnp.mean(o * o, axis=-1, keepdims=True)
        ac_ref[:, sl] = (o * lax.rsqrt(ms + RMS_EPS) * gsb_ref[:, sl]).astype(ac_ref.dtype)
    for r in range(tm // mix_rows):
        rs = slice(r * mix_rows, (r + 1) * mix_rows)
        for g in range(groups):
            sl = slice(g * cg, (g + 1) * cg)
            mixed = _dot(wm_ref[g], vn_ref[rs, sl], precise) + bm_ref[g]
            ocm = u_ref[rs, sl].astype(F32) * mixed
            ms = jnp.mean(ocm * ocm, axis=-1, keepdims=True)
            ac_ref[rs, d_sb + g * cg:d_sb + (g + 1) * cg] = (
                ocm * lax.rsqrt(ms + RMS_EPS) * gcm_ref[:, sl]).astype(ac_ref.dtype)
    y = alpha * x_ref[...] + _dot(ac_ref[...], wout_ref[...], precise)
    mu = jnp.mean(y, axis=-1, keepdims=True)
    yc = y - mu
    var = jnp.mean(yc * yc, axis=-1, keepdims=True)
    x1 = yc * lax.rsqrt(var + LN_EPS) * lng_ref[...] + lnb_ref[...]
    x1_ref[...] = x1
    xpk_ref[...] = _to_slabs(x1, xpk_ref.dtype)


def _merge(o_sb, u_act, vn, x2d, wm, bm, g_sb, g_cm, w_out, ln_g, ln_b, *, precise, alpha, heads,
           groups, tm):
    t, d = x2d.shape
    d_sb, d_cm = o_sb.shape[1], u_act.shape[1]
    mix_rows = wm.shape[1]
    srows = d // LANES
    full = lambda a: pl.BlockSpec(a.shape, lambda i: (0,) * a.ndim)
    rows = lambda a: pl.BlockSpec((tm, a.shape[1]), lambda i: (i, 0))
    return pl.pallas_call(
        functools.partial(_merge_kernel, precise=precise, alpha=alpha, heads=heads, groups=groups,
                          dh=d_sb // heads, cg=d_cm // groups, mix_rows=mix_rows),
        grid=(t // tm,),
        in_specs=[rows(o_sb), rows(u_act), rows(vn), rows(x2d), full(wm), full(bm), full(g_sb),
                  full(g_cm), full(w_out), full(ln_g), full(ln_b)],
        out_specs=[pl.BlockSpec((tm, d), lambda i: (i, 0)),
                   pl.BlockSpec((tm * srows, LANES), lambda i: (i, 0))],
        out_shape=[jax.ShapeDtypeStruct((t, d), F32),
                   jax.ShapeDtypeStruct((t * srows, LANES), BF16)],
        scratch_shapes=[pltpu.VMEM((tm, d_sb + d_cm), F32 if precise else BF16)],
        compiler_params=_cparams("parallel"),
        name="merge_out",
    )(o_sb, u_act, vn, x2d, wm, bm, g_sb, g_cm, w_out, ln_g, ln_b)


def _first_max(vals, index, size, axis):
    m = jnp.max(vals, axis=axis, keepdims=True)
    ix = jnp.min(jnp.where(vals == m, index, size), axis=axis, keepdims=True)
    return m, ix


def _lane_tile(x, n):
    return x[:, :n] if n <= LANES else jnp.concatenate([x] * (n // LANES), axis=1)


def _router_kernel(x_ref, wrt_ref, br_ref, cnt0_ref, idx_ref, gate_ref, rank_ref, cnt_ref, cnt_scr, *,
                   precise):
    n_e = wrt_ref.shape[0]
    tm = x_ref.shape[0]
    eg = n_e // N_EXPERT_GROUPS

    @pl.when(pl.program_id(0) == 0)
    def _():
        cnt_scr[...] = cnt0_ref[...]
    if precise:
        x_hi, x_lo = _split(x_ref[...])
        w_hi, w_lo = _split(wrt_ref[...])
        logits = _dot_nt(w_hi, x_hi) + _dot_nt(w_lo, x_hi) + _dot_nt(w_hi, x_lo)
    else:
        logits = _dot_nt(wrt_ref[...].astype(BF16), x_ref[...].astype(BF16))
    scores = jax.nn.sigmoid(logits)
    sel = scores + br_ref[:, :tm]
    pos = lax.broadcasted_iota(jnp.int32, (eg, tm), 0)
    sel_g, gscore = [], []
    for g in range(N_EXPERT_GROUPS):
        sg = sel[g * eg:(g + 1) * eg]
        m1, i1 = _first_max(sg, pos, eg, 0)
        m2 = jnp.max(jnp.where(pos == i1, NEG_INF, sg), axis=0, keepdims=True)
        sel_g.append(sg)
        gscore.append(m1 + m2)
    cand = []
    for g in range(N_EXPERT_GROUPS):
        beaten = jnp.zeros((1, tm), jnp.int32)
        for o in range(N_EXPERT_GROUPS):
            if o != g:
                wins = (gscore[o] >= gscore[g]) if o < g else (gscore[o] > gscore[g])
                beaten = beaten + wins.astype(jnp.int32)
        cand.append(jnp.where(beaten < TOPK_GROUPS, sel_g[g], NEG_INF))
    cand = jnp.concatenate(cand, axis=0)
    epos = lax.broadcasted_iota(jnp.int32, cand.shape, 0)
    gs, ids = [], []
    onehot = jnp.zeros(cand.shape, F32)
    for k in range(TOP_K):
        _, ei = _first_max(cand, epos, n_e, 0)
        pick = epos == ei
        idx_ref[k:k + 1, :] = ei
        ids.append(ei)
        gs.append(jnp.sum(jnp.where(pick, scores, 0.0), axis=0, keepdims=True))
        onehot = jnp.where(pick, 1.0, onehot)
        cand = jnp.where(pick, NEG_INF, cand)
    total = gs[0]
    for k in range(1, TOP_K):
        total = total + gs[k]
    for k in range(TOP_K):
        gate_ref[k:k + 1, :] = gs[k] / total * ROUTE_SCALE
    tr = lax.broadcasted_iota(jnp.int32, (tm, tm), 0)
    tc = lax.broadcasted_iota(jnp.int32, (tm, tm), 1)
    hot = onehot.astype(BF16)
    before = jnp.dot(hot, (tr < tc).astype(BF16), preferred_element_type=F32) + _lane_tile(cnt_scr[...], tm)
    for k in range(TOP_K):
        rank = jnp.sum(jnp.where(epos == ids[k], before, 0.0), axis=0, keepdims=True)
        rank_ref[k:k + 1, :] = rank.astype(jnp.int32)
    cnt_scr[...] += jnp.dot(hot, jnp.ones((tm, LANES), BF16), preferred_element_type=F32)
    cnt_ref[...] = cnt_scr[...]


def _router(x1, wr_t, br, cnt0, *, precise, tm):
    t, d = x1.shape
    n_e = wr_t.shape[0]
    whole = lambda a: pl.BlockSpec(a.shape, lambda i: (0, 0))
    per_token = pl.BlockSpec((TOP_K, tm), lambda i: (0, i))
    return pl.pallas_call(
        functools.partial(_router_kernel, precise=precise),
        grid=(t // tm,),
        in_specs=[pl.BlockSpec((tm, d), lambda i: (i, 0)), whole(wr_t), whole(br), whole(cnt0)],
        out_specs=[per_token, per_token, per_token, whole(cnt0)],
        out_shape=[jax.ShapeDtypeStruct((TOP_K, t), jnp.int32),
                   jax.ShapeDtypeStruct((TOP_K, t), F32),
                   jax.ShapeDtypeStruct((TOP_K, t), jnp.int32),
                   jax.ShapeDtypeStruct(cnt0.shape, F32)],
        scratch_shapes=[pltpu.VMEM(cnt0.shape, F32)],
        compiler_params=_cparams("arbitrary"),
        name="router",
    )(x1, wr_t, br, cnt0)


def _plan_kernel(idx_ref, rank_ref, cnt_ref, dest_ref, blk_e_ref, nused_ref, *, blk):
    n_e = cnt_ref.shape[0]
    tm = idx_ref.shape[1]
    nb = blk_e_ref.shape[1]
    cnt = cnt_ref[...]
    padded = jnp.floor((cnt + (blk - 1)) * (1.0 / blk)) * blk
    er = lax.broadcasted_iota(jnp.int32, (n_e, n_e), 0)
    ec = lax.broadcasted_iota(jnp.int32, (n_e, n_e), 1)
    pend = jnp.dot((ec <= er).astype(F32), padded, precision=lax.Precision.HIGHEST,
                   preferred_element_type=F32)
    pstart = _lane_tile(pend - padded, tm)
    epos = lax.broadcasted_iota(jnp.int32, (n_e, tm), 0)
    for k in range(TOP_K):
        base = jnp.sum(jnp.where(epos == idx_ref[k:k + 1, :], pstart, 0.0), axis=0, keepdims=True)
        dest_ref[k:k + 1, :] = rank_ref[k:k + 1, :] + base.astype(jnp.int32)
    bidx = lax.broadcasted_iota(jnp.int32, (1, nb), 1)
    ends = jnp.broadcast_to(pend[:, :1], (n_e, nb))
    owner = jnp.sum((ends <= (bidx * blk).astype(F32)).astype(jnp.int32), axis=0, keepdims=True)
    owner = jnp.minimum(owner, n_e - 1)
    n_used = (pend[n_e - 1:n_e, :1] * (1.0 / blk)).astype(jnp.int32)
    last = jnp.sum(jnp.where(bidx == n_used - 1, owner, 0), axis=1, keepdims=True)
    blk_e_ref[...] = jnp.where(bidx < n_used, owner, last)
    nused_ref[...] = jnp.broadcast_to(n_used, nused_ref.shape)


def _plan(idx, rank, cnt, *, blk, n_blocks, tm):
    k, t = idx.shape
    per_token = pl.BlockSpec((k, tm), lambda i: (0, i))
    fixed = lambda shape: pl.BlockSpec(shape, lambda i: (0, 0))
    return pl.pallas_call(
        functools.partial(_plan_kernel, blk=blk),
        grid=(t // tm,),
        in_specs=[per_token, per_token, fixed(cnt.shape)],
        out_specs=[per_token, fixed((1, n_blocks)), fixed((1, LANES))],
        out_shape=[jax.ShapeDtypeStruct((k, t), jnp.int32),
                   jax.ShapeDtypeStruct((1, n_blocks), jnp.int32),
                   jax.ShapeDtypeStruct((1, LANES), jnp.int32)],
        compiler_params=_cparams("arbitrary"),
        name="dispatch_plan",
    )(idx, rank, cnt)


def _scatter_kernel(dest_ref, x_ref, *rest, srows):
    xs_hbm, sem = rest[-2], rest[-1]
    n_tok = x_ref.shape[0] // srows

    def body(r, carry):
        src = x_ref.at[pl.ds(r * srows, srows)]
        for k in range(TOP_K):
            pltpu.make_async_copy(src, xs_hbm.at[pl.ds(dest_ref[0, k, r] * srows, srows)], sem).start(priority=k % 2)
        return carry

    lax.fori_loop(0, n_tok, body, 0)
    for k in range(TOP_K):
        pltpu.make_async_copy(x_ref, xs_hbm.at[pl.ds(0, x_ref.shape[0])], sem).wait()


def _scatter_rows(dest3, xpk, xs_prev, *, n_rows, srows, tm):
    steps = dest3.shape[0]
    in_specs = [pl.BlockSpec((1, TOP_K, tm), lambda i: (i, 0, 0), memory_space=pltpu.SMEM),
                pl.BlockSpec((tm * srows, LANES), lambda i: (i, 0))]
    args = [dest3, xpk]
    aliases = {}
    if xs_prev is not None:
        in_specs.append(pl.BlockSpec(memory_space=pl.ANY))
        args.append(xs_prev)
        aliases = {2: 0}
    return pl.pallas_call(
        functools.partial(_scatter_kernel, srows=srows),
        grid=(steps,),
        in_specs=in_specs,
        out_specs=pl.BlockSpec(memory_space=pl.ANY),
        out_shape=jax.ShapeDtypeStruct((n_rows * srows, LANES), xpk.dtype),
        scratch_shapes=[pltpu.SemaphoreType.DMA(())],
        input_output_aliases=aliases,
        compiler_params=_cparams("arbitrary"),
        name="dispatch_rows",
    )(*args)


def _expert_kernel(blk_e_ref, nused_ref, xs_ref, wg_hbm, wu_hbm, wd_hbm, ys_ref,
                   wg_buf, wu_buf, wd_buf, sem, slot_ref, *, layer, srows):
    s = pl.program_id(0)
    n_used = nused_ref[0]
    blk = xs_ref.shape[0] // srows
    kc = 2 * LANES

    def weight_copies(e, slot):
        return (pltpu.make_async_copy(wg_hbm.at[layer, e], wg_buf.at[slot], sem.at[slot, 0]),
                pltpu.make_async_copy(wu_hbm.at[layer, e], wu_buf.at[slot], sem.at[slot, 1]),
                pltpu.make_async_copy(wd_hbm.at[layer, e], wd_buf.at[slot], sem.at[slot, 2]))

    @pl.when(s < n_used)
    def _():
        e = blk_e_ref[s]
        first = jnp.logical_or(s == 0, e != blk_e_ref[jnp.maximum(s - 1, 0)])

        @pl.when(s == 0)
        def _():
            slot_ref[0] = 0
            for c in weight_copies(e, 0):
                c.start()

        @pl.when(jnp.logical_and(first, s > 0))
        def _():
            slot_ref[0] = 1 - slot_ref[0]

        slot = slot_ref[0]

        @pl.when(first)
        def _():
            for c in weight_copies(e, slot):
                c.wait()
            nxt = lax.while_loop(lambda j: jnp.logical_and(j < n_used, blk_e_ref[jnp.minimum(j, n_used - 1)] == e),
                                 lambda j: j + 1, s + 1)

            @pl.when(nxt < n_used)
            def _():
                for c in weight_copies(blk_e_ref[jnp.minimum(nxt, n_used - 1)], 1 - slot):
                    c.start()

        wg, wu, wd = wg_buf.at[slot], wu_buf.at[slot], wd_buf.at[slot]
        x_rows = _from_slabs(xs_ref[...], srows)
        gate = up = None
        for r in range(x_rows.shape[1] // kc):
            x = x_rows[:, r * kc:(r + 1) * kc]
            g = jnp.dot(x, wg[r * kc:(r + 1) * kc, :].astype(BF16), preferred_element_type=F32)
            u = jnp.dot(x, wu[r * kc:(r + 1) * kc, :].astype(BF16), preferred_element_type=F32)
            gate = g if gate is None else gate + g
            up = u if up is None else up + u
        hid = ((gate * jax.nn.sigmoid(gate)) * up).astype(BF16)
        ys_ref[...] = _to_slabs(jnp.dot(hid, wd[...].astype(BF16), preferred_element_type=F32), ys_ref.dtype)


def _experts(xs, blk_e, n_used, w_eg, w_eu, w_ed, layer, *, blk, srows):
    n_blocks = blk_e.shape[0]
    d, de = w_eg.shape[2], w_eg.shape[3]
    rows = lambda s, be, nu: (jnp.minimum(s, nu[0] - 1), 0)
    hbm = pl.BlockSpec(memory_space=pl.ANY)
    return pl.pallas_call(
        functools.partial(_expert_kernel, layer=layer, srows=srows),
        grid_spec=pltpu.PrefetchScalarGridSpec(
            num_scalar_prefetch=2,
            grid=(n_blocks,),
            in_specs=[pl.BlockSpec((blk * srows, LANES), rows), hbm, hbm, hbm],
            out_specs=pl.BlockSpec((blk * srows, LANES), rows),
            scratch_shapes=[pltpu.VMEM((2, d, de), F32), pltpu.VMEM((2, d, de), F32),
                            pltpu.VMEM((2, de, d), F32), pltpu.SemaphoreType.DMA((2, 3)),
                            pltpu.SMEM((1,), jnp.int32)]),
        out_shape=jax.ShapeDtypeStruct(xs.shape, xs.dtype),
        compiler_params=_cparams("arbitrary"),
        name="experts",
    )(blk_e, n_used, xs, w_eg, w_eu, w_ed)


def _combine_kernel(dest_ref, dest_next_ref, gate_ref, x1_ref, ys_hbm, wsg_ref, wsu_ref, wsd_ref,
                    lng_ref, lnb_ref, x2_ref, ybuf, sem, *, precise, alpha, srows):
    i = pl.program_id(0)
    n_steps = pl.num_programs(0)
    tm = x1_ref.shape[0]
    slot = i % 2

    def issue(ids_ref, to_slot):
        def body(r, carry):
            for k in range(TOP_K):
                pltpu.make_async_copy(ys_hbm.at[pl.ds(ids_ref[0, k, r] * srows, srows)],
                                      ybuf.at[to_slot, k, pl.ds(r * srows, srows)],
                                      sem.at[to_slot]).start(priority=k % 2)
            return carry
        lax.fori_loop(0, tm, body, 0)

    @pl.when(i == 0)
    def _():
        issue(dest_ref, 0)

    @pl.when(i + 1 < n_steps)
    def _():
        issue(dest_next_ref, 1 - slot)

    x1 = x1_ref[...]
    hid = jax.nn.silu(_dot(x1, wsg_ref[...], precise)) * _dot(x1, wsu_ref[...], precise)
    y = alpha * x1 + _dot(hid, wsd_ref[...], precise)
    for k in range(TOP_K):
        pltpu.make_async_copy(ys_hbm.at[pl.ds(0, tm * srows)], ybuf.at[slot, k], sem.at[slot]).wait()
    for k in range(TOP_K):
        y = y + gate_ref[:, k:k + 1] * _from_slabs(ybuf[slot, k], srows).astype(F32)
    mu = jnp.mean(y, axis=-1, keepdims=True)
    yc = y - mu
    var = jnp.mean(yc * yc, axis=-1, keepdims=True)
    x2_ref[...] = yc * lax.rsqrt(var + LN_EPS) * lng_ref[...] + lnb_ref[...]


def _combine(dest3, gates, x1, ys, wsg, wsu, wsd, ln_g, ln_b, *, precise, alpha, srows):
    t, d = x1.shape
    n_steps, _, tm = dest3.shape
    full = lambda a: pl.BlockSpec(a.shape, lambda i: (0,) * a.ndim)
    return pl.pallas_call(
        functools.partial(_combine_kernel, precise=precise, alpha=alpha, srows=srows),
        grid=(n_steps,),
        in_specs=[pl.BlockSpec((1, TOP_K, tm), lambda i: (i, 0, 0), memory_space=pltpu.SMEM),
                  pl.BlockSpec((1, TOP_K, tm), lambda i: (jnp.minimum(i + 1, n_steps - 1), 0, 0),
                               memory_space=pltpu.SMEM),
                  pl.BlockSpec((tm, TOP_K), lambda i: (i, 0)),
                  pl.BlockSpec((tm, d), lambda i: (i, 0)),
                  pl.BlockSpec(memory_space=pl.ANY),
                  full(wsg), full(wsu), full(wsd), full(ln_g), full(ln_b)],
        out_specs=pl.BlockSpec((tm, d), lambda i: (i, 0)),
        out_shape=jax.ShapeDtypeStruct((t, d), F32),
        scratch_shapes=[pltpu.VMEM((2, TOP_K, tm * srows, LANES), ys.dtype), pltpu.SemaphoreType.DMA((2,))],
        compiler_params=_cparams("arbitrary"),
        name="combine",
    )(dest3, dest3, gates, x1, ys, wsg, wsu, wsd, ln_g, ln_b)


def _ple_kernel(x_ref, xcol_ref, p_ref, wg_ref, bg_ref, wp_ref, gp_ref, y_ref, e_ref, *, precise):
    j = pl.program_id(1)
    tn = y_ref.shape[1]

    @pl.when(j == 0)
    def _():
        e = _dot(p_ref[...], wp_ref[...], precise)
        ms = jnp.mean(e * e, axis=-1, keepdims=True)
        e = e * lax.rsqrt(ms + RMS_EPS) * gp_ref[...]
        for jj in range(e_ref.shape[0]):
            e_ref[jj] = e[:, jj * tn:(jj + 1) * tn]

    gate = jax.nn.sigmoid(_dot(x_ref[...], wg_ref[...], precise) + bg_ref[...])
    y_ref[...] = xcol_ref[...] + gate * e_ref[j]


def _ple(x2, p2d, w_gate, b_gate, w_proj, g_proj, *, precise, tm, tn):
    t, d = x2.shape
    dp = p2d.shape[1]
    return pl.pallas_call(
        functools.partial(_ple_kernel, precise=precise),
        grid=(t // tm, d // tn),
        in_specs=[pl.BlockSpec((tm, d), lambda i, j: (i, 0)),
                  pl.BlockSpec((tm, tn), lambda i, j: (i, j)),
                  pl.BlockSpec((tm, dp), lambda i, j: (i, 0)),
                  pl.BlockSpec((d, tn), lambda i, j: (0, j)),
                  pl.BlockSpec((1, tn), lambda i, j: (0, j)),
                  pl.BlockSpec((dp, d), lambda i, j: (0, 0)),
                  pl.BlockSpec((1, d), lambda i, j: (0, 0))],
        out_specs=pl.BlockSpec((tm, tn), lambda i, j: (i, j)),
        out_shape=jax.ShapeDtypeStruct((t, d), F32),
        scratch_shapes=[pltpu.VMEM((d // tn, tm, tn), F32)],
        compiler_params=_cparams("parallel", "arbitrary"),
        name="ple",
    )(x2, x2, p2d, w_gate, b_gate, w_proj, g_proj)


def _tile(n, want):
    t = min(n, want)
    while n % t:
        t -= 1
    return t


def kernel(x_prompt, x_sample, p_prompt, p_sample, cache_k, cache_v, page_table, w_in, b_sb, sg_ln_g, sg_ln_b, w_spatial, b_spatial, g_out_sb, g_out_cm, w_out, ln1_g, ln1_b, w_router, b_router, w_exp_gate, w_exp_up, w_exp_down, w_sh_gate, w_sh_up, w_sh_down, ln2_g, ln2_b, w_ple_gate, b_ple_gate, w_ple_proj, g_ple):
    batch, seq, d = x_prompt.shape
    db, ds, _ = x_sample.shape
    depth = w_in.shape[0]
    n_pool, page, heads, dh = cache_k.shape[1:]
    d_sb = heads * dh
    d_cm = sg_ln_g.shape[1]
    groups, chunk = w_spatial.shape[1], w_spatial.shape[2]
    cg = d_cm // groups
    n_e = w_router.shape[2]
    alpha = float((2 * depth) ** 0.25)
    tp, ts = batch * seq, db * ds
    assert d_sb == d_cm and w_in.shape[2] == 3 * d_sb + 2 * d_cm
    assert seq % chunk == 0 and ds <= min(chunk, SUBLANES) and page == chunk == LANES
    assert d % (2 * LANES) == 0
    srows = d // LANES

    row = lambda a: a.reshape(1, -1)
    tril = jnp.tril(jnp.ones((chunk, chunk), jnp.bool_))
    xp = x_prompt.reshape(tp, d)
    xs = x_sample.reshape(ts, d)
    outs = {k: [] for k in ("kp", "vp", "ks", "vs", "cs")}

    for li in range(depth):
        lng, lnb = row(sg_ln_g[li]), row(sg_ln_b[li])
        w_mix = jnp.where(tril[None], w_spatial[li], 0.0)
        b_mix = jnp.broadcast_to(b_spatial[li][:, :, None], (groups, chunk, cg))
        w_mix_s = jnp.einsum("ab,gts->gatbs", jnp.eye(db, dtype=F32), w_mix[:, :ds, :ds]).reshape(groups, ts, ts)
        b_mix_s = jnp.tile(b_mix[:, :ds], (1, db, 1))
        wr_t = w_router[li].T
        br = jnp.broadcast_to(b_router[li][:, None], (n_e, 256))

        q_scale = LOG2E * dh ** -0.5

        q, k, v, u_act, vn, k_b, v_b = _proj_in(
            xp, w_in[li].astype(BF16), lng, lnb, width=d_sb, group=cg, precise=False, tm=_tile(tp, 512),
            tn=_tile(d_sb, 512), dtypes=(BF16, F32, F32, BF16, BF16, BF16, BF16), q_scale=q_scale)
        o_sb = _attn_prompt(q, k_b, v_b, b_sb[li], batch=batch, seq=seq, heads=heads, tq=_tile(seq, 256),
                            hp=_tile(heads, ATTN_HEADS_PER_STEP), out_dtype=BF16)
        x1p, xpk_p = _merge(o_sb, u_act, vn, xp, w_mix.astype(BF16), b_mix, row(g_out_sb[li]),
                            row(g_out_cm[li]), w_out[li].astype(BF16), row(ln1_g[li]), row(ln1_b[li]),
                            precise=False, alpha=alpha, heads=heads, groups=groups, tm=_tile(tp, 256))
        cnt0 = jnp.zeros((n_e, LANES), F32)
        idx_p, gate_p, rank_p, cnt_p = _router(x1p, wr_t.astype(BF16), br, cnt0, precise=False,
                                               tm=_tile(tp, 256))

        qs, ks, vs, us, vns, _, _ = _proj_in(
            xs, w_in[li], lng, lnb, width=d_sb, group=cg, precise=True, tm=ts, tn=_tile(d_sb, 512),
            dtypes=(F32,) * 7, q_scale=q_scale)
        q_rows = jnp.pad(qs.reshape(db, ds, heads, dh).transpose(0, 2, 1, 3),
                         ((0, 0), (0, 0), (0, SUBLANES - ds), (0, 0))).reshape(db, heads * SUBLANES, dh)
        pad_new = lambda a: jnp.pad(a.reshape(db, ds, heads, dh),
                                    ((0, 0), (0, page - ds), (0, 0), (0, 0))).reshape(db, page * heads, dh)
        o_rows = _attn_decode(q_rows, pad_new(ks), pad_new(vs),
                              cache_k[li].reshape(n_pool, page * heads, dh),
                              cache_v[li].reshape(n_pool, page * heads, dh),
                              page_table, b_sb[li], heads=heads, rows=SUBLANES, page=page,
                              pps=_tile(page_table.shape[1], DECODE_PAGES_PER_STEP))
        o_sb_s = o_rows.reshape(db, heads, SUBLANES, dh)[:, :, :ds].transpose(0, 2, 1, 3).reshape(ts, d_sb)
        x1s, xpk_s = _merge(o_sb_s, us, vns, xs, w_mix_s, b_mix_s, row(g_out_sb[li]), row(g_out_cm[li]),
                            w_out[li], row(ln1_g[li]), row(ln1_b[li]), precise=True, alpha=alpha, heads=heads,
                            groups=groups, tm=ts)
        idx_s, gate_s, rank_s, cnt = _router(x1s, wr_t, br, cnt_p, precise=True, tm=ts)

        blk = EXPERT_ROW_BLOCK
        n_rows = ((tp + ts) * TOP_K + n_e * (blk - 1) + blk - 1) // blk * blk
        tm_c = _tile(tp, 128)
        steps = lambda dest, tm: dest.reshape(TOP_K, -1, tm).transpose(1, 0, 2)
        dest_p, _, _ = _plan(idx_p, rank_p, cnt, blk=blk, n_blocks=n_rows // blk, tm=_tile(tp, 256))
        dest_s, blk_e, n_used = _plan(idx_s, rank_s, cnt, blk=blk, n_blocks=n_rows // blk, tm=ts)
        dest_p, dest_s = steps(dest_p, tm_c), steps(dest_s, ts)
        rows_x = _scatter_rows(dest_p, xpk_p, None, n_rows=n_rows, srows=srows, tm=tm_c)
        rows_x = _scatter_rows(dest_s, xpk_s, rows_x, n_rows=n_rows, srows=srows, tm=ts)
        rows_y = _experts(rows_x, blk_e[0], n_used[0, :1], w_exp_gate, w_exp_up, w_exp_down, li,
                          blk=blk, srows=srows)

        x2p = _combine(dest_p, gate_p.T, x1p, rows_y, w_sh_gate[li].astype(BF16), w_sh_up[li].astype(BF16),
                       w_sh_down[li].astype(BF16), row(ln2_g[li]), row(ln2_b[li]), precise=False, alpha=alpha,
                       srows=srows)
        xp = _ple(x2p, p_prompt[li].reshape(tp, -1), w_ple_gate[li].astype(BF16), row(b_ple_gate[li]),
                  w_ple_proj[li].astype(BF16), row(g_ple[li]), precise=False, tm=_tile(tp, 512),
                  tn=_tile(d, 512))
        x2s = _combine(dest_s, gate_s.T, x1s, rows_y, w_sh_gate[li], w_sh_up[li], w_sh_down[li],
                       row(ln2_g[li]), row(ln2_b[li]), precise=True, alpha=alpha, srows=srows)
        xs = _ple(x2s, p_sample[li].reshape(ts, -1), w_ple_gate[li], row(b_ple_gate[li]), w_ple_proj[li],
                  row(g_ple[li]), precise=True, tm=ts, tn=_tile(d, 512))

        outs["kp"].append(k.reshape(batch, seq, heads, dh))
        outs["vp"].append(v.reshape(batch, seq, heads, dh))
        outs["ks"].append(ks.reshape(db, ds, heads, dh))
        outs["vs"].append(vs.reshape(db, ds, heads, dh))
        outs["cs"].append(vns.reshape(db, ds, groups, cg))

    return (xp.reshape(batch, seq, d), xs.reshape(db, ds, d),
            jnp.stack(outs["kp"], 0), jnp.stack(outs["vp"], 0), jnp.stack(outs["ks"], 0),
            jnp.stack(outs["vs"], 0), jnp.stack(outs["cs"], 0))
```

```python
import functools

import jax
import jax.numpy as jnp
from jax import lax
from jax.experimental import pallas as pl
from jax.experimental.pallas import tpu as pltpu

TOP_K = 8
N_EXPERT_GROUPS = 8
TOPK_GROUPS = 4
ROUTE_SCALE = 2.5
LN_EPS = 1e-5
RMS_EPS = 1e-6

LANES = 128
SUBLANES = 8
VMEM_LIMIT_BYTES = 56 * 1024 * 1024

EXPERT_ROW_BLOCK = 256
ATTN_HEADS_PER_STEP = 4
ATTN_ROW_CHUNK = 128
DECODE_PAGES_PER_STEP = 8
F32 = jnp.float32
BF16 = jnp.bfloat16
NEG_INF = float("-inf")


def _cparams(*sem):
    return pltpu.CompilerParams(dimension_semantics=sem, vmem_limit_bytes=VMEM_LIMIT_BYTES)


def _dot(a, b, precise):
    if precise:
        return jnp.dot(a.astype(F32), b.astype(F32), precision=lax.Precision.HIGHEST,
                       preferred_element_type=F32)
    return jnp.dot(a.astype(BF16), b.astype(BF16), preferred_element_type=F32)


def _dot_nt(a, b):
    return lax.dot_general(a, b, (((1,), (1,)), ((), ())), preferred_element_type=F32)


def _split(x):
    hi = x.astype(BF16)
    lo = (x - hi.astype(F32)).astype(BF16)
    return hi, lo


def _gelu(x):
    c = 0.7978845608028654
    return x * (0.5 * (1.0 + jnp.tanh(c * (x + 0.044715 * (x * x * x)))))


LOG2E = 1.4426950408889634


def _suffix_matrix(n):
    r = lax.broadcasted_iota(jnp.int32, (n, n), 0)
    c = lax.broadcasted_iota(jnp.int32, (n, n), 1)
    return (r > c).astype(BF16)


def _stick_terms(z2):
    drop = jnp.maximum(z2, 0.0) + jnp.log2(1.0 + jnp.exp2(-jnp.abs(z2)))
    return drop, z2 - drop


def _suffix_sums(drop, suffix):
    hi, lo = _split(drop)
    return (jnp.dot(hi, suffix, preferred_element_type=F32)
            + jnp.dot(lo, suffix, preferred_element_type=F32))


_PROJ_SEGMENT = (0, 1, 2, 3, 4, 1, 2)


def _proj_in_kernel(x_ref, w_ref, lng_ref, lnb_ref, q_ref, k_ref, v_ref, u_ref, vn_ref, kb_ref, vb_ref, *,
                    precise, n_per, group, q_scale):
    seg = pl.program_id(1) // n_per
    acc = _dot(x_ref[...], w_ref[...], precise)

    @pl.when(seg == 0)
    def _():
        q_ref[...] = (acc * q_scale).astype(q_ref.dtype)

    def store_heads(ref, first_step):
        tm, hpt = acc.shape[0], acc.shape[1] // LANES
        for jj in range(n_per):
            @pl.when(pl.program_id(1) == first_step + jj)
            def _():
                for hh in range(hpt):
                    ref[pl.ds(jj * hpt + hh, tm, stride=n_per * hpt), :] = (
                        acc[:, hh * LANES:(hh + 1) * LANES].astype(ref.dtype))

    @pl.when(seg == 1)
    def _():
        store_heads(k_ref, n_per)
        kb_ref[...] = acc.astype(kb_ref.dtype)

    @pl.when(seg == 2)
    def _():
        store_heads(v_ref, 2 * n_per)
        vb_ref[...] = acc.astype(vb_ref.dtype)

    @pl.when(seg == 3)
    def _():
        u_ref[...] = _gelu(acc).astype(u_ref.dtype)

    @pl.when(seg == 4)
    def _():
        g = _gelu(acc)
        for c in range(acc.shape[1] // group):
            sl = slice(c * group, (c + 1) * group)
            xg = g[:, sl]
            mu = jnp.mean(xg, axis=-1, keepdims=True)
            xc = xg - mu
            var = jnp.mean(xc * xc, axis=-1, keepdims=True)
            y = xc * lax.rsqrt(var + LN_EPS) * lng_ref[:, sl] + lnb_ref[:, sl]
            vn_ref[:, sl] = y.astype(vn_ref.dtype)


def _proj_in(x2d, w, ln_g, ln_b, *, width, group, precise, tm, tn, dtypes, q_scale):
    t, d = x2d.shape
    n_per = width // tn
    heads = width // LANES
    grid = (t // tm, 5 * n_per)

    def out_map(seg):
        return lambda i, j: (i, jnp.clip(j - seg * n_per, 0, n_per - 1))

    ln_map = lambda i, j: (0, jnp.clip(j - 4 * n_per, 0, n_per - 1))
    return pl.pallas_call(
        functools.partial(_proj_in_kernel, precise=precise, n_per=n_per, group=group, q_scale=q_scale),
        grid=grid,
        in_specs=[pl.BlockSpec((tm, d), lambda i, j: (i, 0)),
                  pl.BlockSpec((d, tn), lambda i, j: (0, j)),
                  pl.BlockSpec((1, tn), ln_map),
                  pl.BlockSpec((1, tn), ln_map)],
        out_specs=[pl.BlockSpec((tm * heads, LANES), lambda i, j: (i, 0)) if o in (1, 2)
                   else pl.BlockSpec((tm, tn), out_map(seg)) for o, seg in enumerate(_PROJ_SEGMENT)],
        out_shape=[jax.ShapeDtypeStruct((t * heads, LANES) if o in (1, 2) else (t, width), dt)
                   for o, dt in enumerate(dtypes)],
        compiler_params=_cparams("parallel", "arbitrary"),
        name="proj_in",
    )(x2d, w, ln_g, ln_b)


def _attn_prompt_kernel(bsb_ref, q_ref, kb_ref, vb_ref, o_ref, z_ref, w_ref, *, tq, dh, hp, rc):
    hg = pl.program_id(1)
    i = pl.program_id(2)
    heads = range(hp)
    cols = [slice(h * dh, (h + 1) * dh) for h in heads]
    bias = [bsb_ref[hg * hp + h] * LOG2E for h in heads]
    q = [q_ref[:, cols[h]] for h in heads]
    suffix = _suffix_matrix(tq)
    row = lax.broadcasted_iota(jnp.int32, (tq, tq), 0)
    col = lax.broadcasted_iota(jnp.int32, (tq, tq), 1)
    causal = col < row

    def rows(kb):
        return pl.ds(pl.multiple_of(kb * tq, tq), tq)

    def scores(kb):
        return [_dot_nt(q[h], kb_ref[rows(kb), cols[h]]) + bias[h] for h in heads]

    def values(kb, accs):
        return [accs[h] + jnp.dot(w_ref[h], vb_ref[rows(kb), cols[h]], preferred_element_type=F32)
                for h in heads]

    def weights(carries, masked):
        totals = [[] for _ in heads]
        for c in range(tq // rc):
            rs = slice(c * rc, (c + 1) * rc)
            terms = [_stick_terms(z_ref[h, rs, :]) for h in heads]
            drops = [jnp.where(causal[rs], t[0], 0.0) if masked else t[0] for t in terms]
            sums = [_suffix_sums(d, suffix) for d in drops]
            for h in heads:
                w = jnp.exp2(terms[h][1] - sums[h] - carries[h][rs])
                if masked:
                    w = jnp.where(causal[rs], w, 0.0)
                w_ref[h, rs, :] = w.astype(BF16)
                totals[h].append(jnp.sum(drops[h], axis=-1, keepdims=True))
        return [carries[h] + jnp.concatenate(totals[h], axis=0) for h in heads]

    def put_scores(zs):
        for h in heads:
            z_ref[h] = zs[h]

    put_scores(scores(i))
    z_next = scores(jnp.maximum(i - 1, 0))
    carries = weights([jnp.zeros((tq, 1), F32) for _ in heads], True)
    put_scores(z_next)
    accs = [jnp.zeros((tq, dh), F32) for _ in heads]

    def step(n, state):
        accs, carries = state
        kb = i - n
        accs = values(kb + 1, accs)
        z_next = scores(jnp.maximum(kb - 1, 0))
        carries = weights(carries, False)
        put_scores(z_next)
        return accs, carries

    accs, carries = lax.fori_loop(1, i + 1, step, (accs, carries))
    accs = values(0, accs)
    for h in heads:
        o_ref[:, cols[h]] = accs[h].astype(o_ref.dtype)


def _attn_prompt(q, k, v, b_sb, *, batch, seq, heads, tq, hp, out_dtype, rc=ATTN_ROW_CHUNK):
    t, width = q.shape
    dh = width // heads
    nq = seq // tq
    grid = (batch, heads // hp, nq)
    wide = hp * dh
    return pl.pallas_call(
        functools.partial(_attn_prompt_kernel, tq=tq, dh=dh, hp=hp, rc=min(rc, tq)),
        grid_spec=pltpu.PrefetchScalarGridSpec(
            num_scalar_prefetch=0,
            grid=grid,
            in_specs=[pl.BlockSpec(memory_space=pltpu.SMEM),
                      pl.BlockSpec((tq, wide), lambda b, h, i: (b * nq + i, h)),
                      pl.BlockSpec((seq, wide), lambda b, h, i: (b, h)),
                      pl.BlockSpec((seq, wide), lambda b, h, i: (b, h))],
            out_specs=pl.BlockSpec((tq, wide), lambda b, h, i: (b * nq + i, h)),
            scratch_shapes=[pltpu.VMEM((hp, tq, tq), F32), pltpu.VMEM((hp, tq, tq), BF16)]),
        out_shape=jax.ShapeDtypeStruct((t, width), out_dtype),
        compiler_params=_cparams("arbitrary", "arbitrary", "arbitrary"),
        name="attn_prompt",
    )(b_sb, q, k, v)


def _attn_decode_kernel(pt_ref, bsb_ref, q_ref, kn_ref, vn_ref, *refs, heads, rows, page, pps):
    k_refs, v_refs = refs[:pps], refs[pps:2 * pps]
    o_ref, carry_ref = refs[2 * pps], refs[2 * pps + 1]
    p = pl.program_id(1)
    suffix = _suffix_matrix(page)
    n = heads * rows
    row = lax.broadcasted_iota(jnp.int32, (n, page), 0)
    col = lax.broadcasted_iota(jnp.int32, (n, page), 1)
    causal = col < (row % rows)
    bias = jnp.concatenate([jnp.full((rows, page), bsb_ref[h] * LOG2E, F32) for h in range(heads)], axis=0)
    q_hi, q_lo = _split(q_ref[0])
    hs = lambda x, h: x[h * rows:(h + 1) * rows]
    zero = jnp.zeros((rows, q_hi.shape[1]), BF16)
    mm = functools.partial(jnp.dot, preferred_element_type=F32)
    q_pairs = []
    for a in range(0, heads, 2):
        qa_hi, qa_lo, qb_hi, qb_lo = hs(q_hi, a), hs(q_lo, a), hs(q_hi, a + 1), hs(q_lo, a + 1)
        wide = lambda left, right: jnp.concatenate([left, right], axis=1)
        full = jnp.concatenate([wide(qa_hi, zero), wide(qa_lo, zero), wide(zero, qb_hi), wide(zero, qb_lo)], axis=0)
        top = jnp.concatenate([wide(qa_hi, zero), wide(zero, qb_hi)], axis=0)
        q_pairs.append((full, top))

    def head_pair(ref, a, axis):
        xa = _split(ref[0, pl.ds(a, page, stride=heads), :])
        xb = _split(ref[0, pl.ds(a + 1, page, stride=heads), :])
        return (jnp.concatenate([xa[0], xb[0]], axis=axis), jnp.concatenate([xa[1], xb[1]], axis=axis))

    def blocks(kv, masked):
        zs = []
        for k_ref, _ in kv:
            zh = []
            for pi, a in enumerate(range(0, heads, 2)):
                k_hi, k_lo = head_pair(k_ref, a, 1)
                r1 = _dot_nt(q_pairs[pi][0], k_hi)
                r2 = _dot_nt(q_pairs[pi][1], k_lo)
                zh.append(hs(r1, 0) + hs(r1, 1) + hs(r2, 0))
                zh.append(hs(r1, 2) + hs(r1, 3) + hs(r2, 1))
            zs.append(jnp.concatenate(zh, axis=0) + bias)
        terms = [_stick_terms(z) for z in zs]
        drops = [jnp.where(causal, t[0], 0.0) if masked else t[0] for t in terms]
        sums = [_suffix_sums(d, suffix) for d in drops]
        carry = carry_ref[...]
        for (_, v_ref), t, d, s in zip(kv, terms, drops, sums):
            w = jnp.exp2(t[1] - s - carry)
            if masked:
                w = jnp.where(causal, w, 0.0)
            carry = carry + jnp.sum(d, axis=-1, keepdims=True)
            w_hi, w_lo = _split(w)
            for a in range(0, heads, 2):
                v_hi, v_lo = head_pair(v_ref, a, 1)
                r1 = mm(jnp.concatenate([hs(w_hi, a), hs(w_lo, a), hs(w_hi, a + 1), hs(w_lo, a + 1)], axis=0), v_hi)
                r2 = mm(jnp.concatenate([hs(w_hi, a), hs(w_hi, a + 1)], axis=0), v_lo)
                dh = r1.shape[1] // 2
                o_ref[0, a * rows:(a + 1) * rows, :] += (hs(r1, 0) + hs(r1, 1) + hs(r2, 0))[:, :dh]
                o_ref[0, (a + 1) * rows:(a + 2) * rows, :] += (hs(r1, 2) + hs(r1, 3) + hs(r2, 1))[:, dh:]
        carry_ref[...] = carry

    @pl.when(p == 0)
    def _():
        o_ref[...] = jnp.zeros_like(o_ref)
        carry_ref[...] = jnp.zeros_like(carry_ref)
        blocks([(kn_ref, vn_ref)], True)

    blocks(list(zip(k_refs, v_refs)), False)


def _attn_decode(q_rows, k_new, v_new, k_pool, v_pool, page_table, b_sb, *, heads, rows, page, pps):
    db, n, dh = q_rows.shape
    n_pages = page_table.shape[1]
    blk = (1, page * heads, dh)
    seq_spec = lambda shape: pl.BlockSpec(shape, lambda b, p, pt, bs: (b, 0, 0))

    def pool_spec(j):
        return pl.BlockSpec(blk, lambda b, p, pt, bs: (pt[b, n_pages - 1 - (p * pps + j)], 0, 0))

    return pl.pallas_call(
        functools.partial(_attn_decode_kernel, heads=heads, rows=rows, page=page, pps=pps),
        grid_spec=pltpu.PrefetchScalarGridSpec(
            num_scalar_prefetch=2,
            grid=(db, n_pages // pps),
            in_specs=[seq_spec((1, n, dh)), seq_spec(blk), seq_spec(blk)]
                     + [pool_spec(j) for j in range(pps)] * 2,
            out_specs=seq_spec((1, n, dh)),
            scratch_shapes=[pltpu.VMEM((n, LANES), F32)]),
        out_shape=jax.ShapeDtypeStruct((db, n, dh), F32),
        compiler_params=_cparams("parallel", "arbitrary"),
        name="attn_decode",
    )(page_table, b_sb, q_rows, k_new, v_new, *([k_pool] * pps), *([v_pool] * pps))


def _to_slabs(x, dtype):
    n, d = x.shape
    return x.astype(dtype).reshape(n * (d // LANES), LANES)


def _from_slabs(s, srows):
    return s.reshape(s.shape[0] // srows, srows * LANES)


def _merge_kernel(osb_ref, u_ref, vn_ref, x_ref, wm_ref, bm_ref, gsb_ref, gcm_ref, wout_ref,
                  lng_ref, lnb_ref, x1_ref, xpk_ref, ac_ref, *, precise, alpha, heads, groups, dh, cg,
                  mix_rows):
    tm = x_ref.shape[0]
    d_sb = heads * dh
    for h in range(heads):
        sl = slice(h * dh, (h + 1) * dh)
        o = osb_ref[:, sl].astype(F32)
        ms = jnp.mean(o * o, axis=-1, keepdims=True)
        ac_ref[:, sl] = (o * lax.rsqrt(ms + RMS_EPS) * gsb_ref[:, sl]).astype(ac_ref.dtype)
    for r in range(tm // mix_rows):
        rs = slice(r * mix_rows, (r + 1) * mix_rows)
        for g in range(groups):
            sl = slice(g * cg, (g + 1) * cg)
            mixed = _dot(wm_ref[g], vn_ref[rs, sl], precise) + bm_ref[g]
            ocm = u_ref[rs, sl].astype(F32) * mixed
            ms = jnp.mean(ocm * ocm, axis=-1, keepdims=True)
            ac_ref[rs, d_sb + g * cg:d_sb + (g + 1) * cg] = (
                ocm * lax.rsqrt(ms + RMS_EPS) * gcm_ref[:, sl]).astype(ac_ref.dtype)
    y = alpha * x_ref[...] + _dot(ac_ref[...], wout_ref[...], precise)
    mu = jnp.mean(y, axis=-1, keepdims=True)
    yc = y - mu
    var = jnp.mean(yc * yc, axis=-1, keepdims=True)
    x1 = yc * lax.rsqrt(var + LN_EPS) * lng_ref[...] + lnb_ref[...]
    x1_ref[...] = x1
    xpk_ref[...] = _to_slabs(x1, xpk_ref.dtype)


def _merge(o_sb, u_act, vn, x2d, wm, bm, g_sb, g_cm, w_out, ln_g, ln_b, *, precise, alpha, heads,
           groups, tm):
    t, d = x2d.shape
    d_sb, d_cm = o_sb.shape[1], u_act.shape[1]
    mix_rows = wm.shape[1]
    srows = d // LANES
    full = lambda a: pl.BlockSpec(a.shape, lambda i: (0,) * a.ndim)
    rows = lambda a: pl.BlockSpec((tm, a.shape[1]), lambda i: (i, 0))
    return pl.pallas_call(
        functools.partial(_merge_kernel, precise=precise, alpha=alpha, heads=heads, groups=groups,
                          dh=d_sb // heads, cg=d_cm // groups, mix_rows=mix_rows),
        grid=(t // tm,),
        in_specs=[rows(o_sb), rows(u_act), rows(vn), rows(x2d), full(wm), full(bm), full(g_sb),
                  full(g_cm), full(w_out), full(ln_g), full(ln_b)],
        out_specs=[pl.BlockSpec((tm, d), lambda i: (i, 0)),
                   pl.BlockSpec((tm * srows, LANES), lambda i: (i, 0))],
        out_shape=[jax.ShapeDtypeStruct((t, d), F32),
                   jax.ShapeDtypeStruct((t * srows, LANES), BF16)],
        scratch_shapes=[pltpu.VMEM((tm, d_sb + d_cm), F32 if precise else BF16)],
        compiler_params=_cparams("parallel"),
        name="merge_out",
    )(o_sb, u_act, vn, x2d, wm, bm, g_sb, g_cm, w_out, ln_g, ln_b)


def _first_max(vals, index, size, axis):
    m = jnp.max(vals, axis=axis, keepdims=True)
    ix = jnp.min(jnp.where(vals == m, index, size), axis=axis, keepdims=True)
    return m, ix


def _lane_tile(x, n):
    return x[:, :n] if n <= LANES else jnp.concatenate([x] * (n // LANES), axis=1)


def _router_kernel(x_ref, wrt_ref, br_ref, cnt0_ref, idx_ref, gate_ref, rank_ref, cnt_ref, cnt_scr, *,
                   precise):
    n_e = wrt_ref.shape[0]
    tm = x_ref.shape[0]
    eg = n_e // N_EXPERT_GROUPS

    @pl.when(pl.program_id(0) == 0)
    def _():
        cnt_scr[...] = cnt0_ref[...]
    if precise:
        x_hi, x_lo = _split(x_ref[...])
        w_hi, w_lo = _split(wrt_ref[...])
        logits = _dot_nt(w_hi, x_hi) + _dot_nt(w_lo, x_hi) + _dot_nt(w_hi, x_lo)
    else:
        logits = _dot_nt(wrt_ref[...].astype(BF16), x_ref[...].astype(BF16))
    scores = jax.nn.sigmoid(logits)
    sel = scores + br_ref[:, :tm]
    pos = lax.broadcasted_iota(jnp.int32, (eg, tm), 0)
    sel_g, gscore = [], []
    for g in range(N_EXPERT_GROUPS):
        sg = sel[g * eg:(g + 1) * eg]
        m1, i1 = _first_max(sg, pos, eg, 0)
        m2 = jnp.max(jnp.where(pos == i1, NEG_INF, sg), axis=0, keepdims=True)
        sel_g.append(sg)
        gscore.append(m1 + m2)
    cand = []
    for g in range(N_EXPERT_GROUPS):
        beaten = jnp.zeros((1, tm), jnp.int32)
        for o in range(N_EXPERT_GROUPS):
            if o != g:
                wins = (gscore[o] >= gscore[g]) if o < g else (gscore[o] > gscore[g])
                beaten = beaten + wins.astype(jnp.int32)
        cand.append(jnp.where(beaten < TOPK_GROUPS, sel_g[g], NEG_INF))
    cand = jnp.concatenate(cand, axis=0)
    epos = lax.broadcasted_iota(jnp.int32, cand.shape, 0)
    gs, ids = [], []
    onehot = jnp.zeros(cand.shape, F32)
    for k in range(TOP_K):
        _, ei = _first_max(cand, epos, n_e, 0)
        pick = epos == ei
        idx_ref[k:k + 1, :] = ei
        ids.append(ei)
        gs.append(jnp.sum(jnp.where(pick, scores, 0.0), axis=0, keepdims=True))
        onehot = jnp.where(pick, 1.0, onehot)
        cand = jnp.where(pick, NEG_INF, cand)
    total = gs[0]
    for k in range(1, TOP_K):
        total = total + gs[k]
    for k in range(TOP_K):
        gate_ref[k:k + 1, :] = gs[k] / total * ROUTE_SCALE
    tr = lax.broadcasted_iota(jnp.int32, (tm, tm), 0)
    tc = lax.broadcasted_iota(jnp.int32, (tm, tm), 1)
    hot = onehot.astype(BF16)
    before = jnp.dot(hot, (tr < tc).astype(BF16), preferred_element_type=F32) + _lane_tile(cnt_scr[...], tm)
    for k in range(TOP_K):
        rank = jnp.sum(jnp.where(epos == ids[k], before, 0.0), axis=0, keepdims=True)
        rank_ref[k:k + 1, :] = rank.astype(jnp.int32)
    cnt_scr[...] += jnp.dot(hot, jnp.ones((tm, LANES), BF16), preferred_element_type=F32)
    cnt_ref[...] = cnt_scr[...]


def _router(x1, wr_t, br, cnt0, *, precise, tm):
    t, d = x1.shape
    n_e = wr_t.shape[0]
    whole = lambda a: pl.BlockSpec(a.shape, lambda i: (0, 0))
    per_token = pl.BlockSpec((TOP_K, tm), lambda i: (0, i))
    return pl.pallas_call(
        functools.partial(_router_kernel, precise=precise),
        grid=(t // tm,),
        in_specs=[pl.BlockSpec((tm, d), lambda i: (i, 0)), whole(wr_t), whole(br), whole(cnt0)],
        out_specs=[per_token, per_token, per_token, whole(cnt0)],
        out_shape=[jax.ShapeDtypeStruct((TOP_K, t), jnp.int32),
                   jax.ShapeDtypeStruct((TOP_K, t), F32),
                   jax.ShapeDtypeStruct((TOP_K, t), jnp.int32),
                   jax.ShapeDtypeStruct(cnt0.shape, F32)],
        scratch_shapes=[pltpu.VMEM(cnt0.shape, F32)],
        compiler_params=_cparams("arbitrary"),
        name="router",
    )(x1, wr_t, br, cnt0)


def _plan_kernel(idx_ref, rank_ref, cnt_ref, dest_ref, blk_e_ref, nused_ref, *, blk):
    n_e = cnt_ref.shape[0]
    tm = idx_ref.shape[1]
    nb = blk_e_ref.shape[1]
    cnt = cnt_ref[...]
    padded = jnp.floor((cnt + (blk - 1)) * (1.0 / blk)) * blk
    er = lax.broadcasted_iota(jnp.int32, (n_e, n_e), 0)
    ec = lax.broadcasted_iota(jnp.int32, (n_e, n_e), 1)
    pend = jnp.dot((ec <= er).astype(F32), padded, precision=lax.Precision.HIGHEST,
                   preferred_element_type=F32)
    pstart = _lane_tile(pend - padded, tm)
    epos = lax.broadcasted_iota(jnp.int32, (n_e, tm), 0)
    for k in range(TOP_K):
        base = jnp.sum(jnp.where(epos == idx_ref[k:k + 1, :], pstart, 0.0), axis=0, keepdims=True)
        dest_ref[k:k + 1, :] = rank_ref[k:k + 1, :] + base.astype(jnp.int32)
    bidx = lax.broadcasted_iota(jnp.int32, (1, nb), 1)
    ends = jnp.broadcast_to(pend[:, :1], (n_e, nb))
    owner = jnp.sum((ends <= (bidx * blk).astype(F32)).astype(jnp.int32), axis=0, keepdims=True)
    owner = jnp.minimum(owner, n_e - 1)
    n_used = (pend[n_e - 1:n_e, :1] * (1.0 / blk)).astype(jnp.int32)
    last = jnp.sum(jnp.where(bidx == n_used - 1, owner, 0), axis=1, keepdims=True)
    blk_e_ref[...] = jnp.where(bidx < n_used, owner, last)
    nused_ref[...] = jnp.broadcast_to(n_used, nused_ref.shape)


def _plan(idx, rank, cnt, *, blk, n_blocks, tm):
    k, t = idx.shape
    per_token = pl.BlockSpec((k, tm), lambda i: (0, i))
    fixed = lambda shape: pl.BlockSpec(shape, lambda i: (0, 0))
    return pl.pallas_call(
        functools.partial(_plan_kernel, blk=blk),
        grid=(t // tm,),
        in_specs=[per_token, per_token, fixed(cnt.shape)],
        out_specs=[per_token, fixed((1, n_blocks)), fixed((1, LANES))],
        out_shape=[jax.ShapeDtypeStruct((k, t), jnp.int32),
                   jax.ShapeDtypeStruct((1, n_blocks), jnp.int32),
                   jax.ShapeDtypeStruct((1, LANES), jnp.int32)],
        compiler_params=_cparams("arbitrary"),
        name="dispatch_plan",
    )(idx, rank, cnt)


def _scatter_kernel(dest_ref, x_ref, *rest, srows):
    xs_hbm, sem = rest[-2], rest[-1]
    n_tok = x_ref.shape[0] // srows

    def body(r, carry):
        src = x_ref.at[pl.ds(r * srows, srows)]
        for k in range(TOP_K):
            pltpu.make_async_copy(src, xs_hbm.at[pl.ds(dest_ref[0, k, r] * srows, srows)], sem).start(priority=k % 2)
        return carry

    lax.fori_loop(0, n_tok, body, 0)
    for k in range(TOP_K):
        pltpu.make_async_copy(x_ref, xs_hbm.at[pl.ds(0, x_ref.shape[0])], sem).wait()


def _scatter_rows(dest3, xpk, xs_prev, *, n_rows, srows, tm):
    steps = dest3.shape[0]
    in_specs = [pl.BlockSpec((1, TOP_K, tm), lambda i: (i, 0, 0), memory_space=pltpu.SMEM),
                pl.BlockSpec((tm * srows, LANES), lambda i: (i, 0))]
    args = [dest3, xpk]
    aliases = {}
    if xs_prev is not None:
        in_specs.append(pl.BlockSpec(memory_space=pl.ANY))
        args.append(xs_prev)
        aliases = {2: 0}
    return pl.pallas_call(
        functools.partial(_scatter_kernel, srows=srows),
        grid=(steps,),
        in_specs=in_specs,
        out_specs=pl.BlockSpec(memory_space=pl.ANY),
        out_shape=jax.ShapeDtypeStruct((n_rows * srows, LANES), xpk.dtype),
        scratch_shapes=[pltpu.SemaphoreType.DMA(())],
        input_output_aliases=aliases,
        compiler_params=_cparams("arbitrary"),
        name="dispatch_rows",
    )(*args)


def _expert_kernel(blk_e_ref, nused_ref, xs_ref, wg_hbm, wu_hbm, wd_hbm, ys_ref,
                   wg_buf, wu_buf, wd_buf, sem, slot_ref, *, layer, srows):
    s = pl.program_id(0)
    n_used = nused_ref[0]
    blk = xs_ref.shape[0] // srows
    kc = 2 * LANES

    def weight_copies(e, slot):
        return (pltpu.make_async_copy(wg_hbm.at[layer, e], wg_buf.at[slot], sem.at[slot, 0]),
                pltpu.make_async_copy(wu_hbm.at[layer, e], wu_buf.at[slot], sem.at[slot, 1]),
                pltpu.make_async_copy(wd_hbm.at[layer, e], wd_buf.at[slot], sem.at[slot, 2]))

    @pl.when(s < n_used)
    def _():
        e = blk_e_ref[s]
        first = jnp.logical_or(s == 0, e != blk_e_ref[jnp.maximum(s - 1, 0)])

        @pl.when(s == 0)
        def _():
            slot_ref[0] = 0
            for c in weight_copies(e, 0):
                c.start()

        @pl.when(jnp.logical_and(first, s > 0))
        def _():
            slot_ref[0] = 1 - slot_ref[0]

        slot = slot_ref[0]

        @pl.when(first)
        def _():
            for c in weight_copies(e, slot):
                c.wait()
            nxt = lax.while_loop(lambda j: jnp.logical_and(j < n_used, blk_e_ref[jnp.minimum(j, n_used - 1)] == e),
                                 lambda j: j + 1, s + 1)

            @pl.when(nxt < n_used)
            def _():
                for c in weight_copies(blk_e_ref[jnp.minimum(nxt, n_used - 1)], 1 - slot):
                    c.start()

        wg, wu, wd = wg_buf.at[slot], wu_buf.at[slot], wd_buf.at[slot]
        x_rows = _from_slabs(xs_ref[...], srows)
        gate = up = None
        for r in range(x_rows.shape[1] // kc):
            x = x_rows[:, r * kc:(r + 1) * kc]
            g = jnp.dot(x, wg[r * kc:(r + 1) * kc, :].astype(BF16), preferred_element_type=F32)
            u = jnp.dot(x, wu[r * kc:(r + 1) * kc, :].astype(BF16), preferred_element_type=F32)
            gate = g if gate is None else gate + g
            up = u if up is None else up + u
        hid = ((gate * jax.nn.sigmoid(gate)) * up).astype(BF16)
        ys_ref[...] = _to_slabs(jnp.dot(hid, wd[...].astype(BF16), preferred_element_type=F32), ys_ref.dtype)


def _experts(xs, blk_e, n_used, w_eg, w_eu, w_ed, layer, *, blk, srows):
    n_blocks = blk_e.shape[0]
    d, de = w_eg.shape[2], w_eg.shape[3]
    rows = lambda s, be, nu: (jnp.minimum(s, nu[0] - 1), 0)
    hbm = pl.BlockSpec(memory_space=pl.ANY)
    return pl.pallas_call(
        functools.partial(_expert_kernel, layer=layer, srows=srows),
        grid_spec=pltpu.PrefetchScalarGridSpec(
            num_scalar_prefetch=2,
            grid=(n_blocks,),
            in_specs=[pl.BlockSpec((blk * srows, LANES), rows), hbm, hbm, hbm],
            out_specs=pl.BlockSpec((blk * srows, LANES), rows),
            scratch_shapes=[pltpu.VMEM((2, d, de), F32), pltpu.VMEM((2, d, de), F32),
                            pltpu.VMEM((2, de, d), F32), pltpu.SemaphoreType.DMA((2, 3)),
                            pltpu.SMEM((1,), jnp.int32)]),
        out_shape=jax.ShapeDtypeStruct(xs.shape, xs.dtype),
        compiler_params=_cparams("arbitrary"),
        name="experts",
    )(blk_e, n_used, xs, w_eg, w_eu, w_ed)


def _combine_kernel(dest_ref, dest_next_ref, gate_ref, x1_ref, ys_hbm, wsg_ref, wsu_ref, wsd_ref,
                    lng_ref, lnb_ref, x2_ref, ybuf, sem, *, precise, alpha, srows):
    i = pl.program_id(0)
    n_steps = pl.num_programs(0)
    tm = x1_ref.shape[0]
    slot = i % 2

    def issue(ids_ref, to_slot):
        def body(r, carry):
            for k in range(TOP_K):
                pltpu.make_async_copy(ys_hbm.at[pl.ds(ids_ref[0, k, r] * srows, srows)],
                                      ybuf.at[to_slot, k, pl.ds(r * srows, srows)],
                                      sem.at[to_slot]).start(priority=k % 2)
            return carry
        lax.fori_loop(0, tm, body, 0)

    @pl.when(i == 0)
    def _():
        issue(dest_ref, 0)

    @pl.when(i + 1 < n_steps)
    def _():
        issue(dest_next_ref, 1 - slot)

    x1 = x1_ref[...]
    hid = jax.nn.silu(_dot(x1, wsg_ref[...], precise)) * _dot(x1, wsu_ref[...], precise)
    y = alpha * x1 + _dot(hid, wsd_ref[...], precise)
    for k in range(TOP_K):
        pltpu.make_async_copy(ys_hbm.at[pl.ds(0, tm * srows)], ybuf.at[slot, k], sem.at[slot]).wait()
    for k in range(TOP_K):
        y = y + gate_ref[:, k:k + 1] * _from_slabs(ybuf[slot, k], srows).astype(F32)
    mu = jnp.mean(y, axis=-1, keepdims=True)
    yc = y - mu
    var = jnp.mean(yc * yc, axis=-1, keepdims=True)
    x2_ref[...] = yc * lax.rsqrt(var + LN_EPS) * lng_ref[...] + lnb_ref[...]


def _combine(dest3, gates, x1, ys, wsg, wsu, wsd, ln_g, ln_b, *, precise, alpha, srows):
    t, d = x1.shape
    n_steps, _, tm = dest3.shape
    full = lambda a: pl.BlockSpec(a.shape, lambda i: (0,) * a.ndim)
    return pl.pallas_call(
        functools.partial(_combine_kernel, precise=precise, alpha=alpha, srows=srows),
        grid=(n_steps,),
        in_specs=[pl.BlockSpec((1, TOP_K, tm), lambda i: (i, 0, 0), memory_space=pltpu.SMEM),
                  pl.BlockSpec((1, TOP_K, tm), lambda i: (jnp.minimum(i + 1, n_steps - 1), 0, 0),
                               memory_space=pltpu.SMEM),
                  pl.BlockSpec((tm, TOP_K), lambda i: (i, 0)),
                  pl.BlockSpec((tm, d), lambda i: (i, 0)),
                  pl.BlockSpec(memory_space=pl.ANY),
                  full(wsg), full(wsu), full(wsd), full(ln_g), full(ln_b)],
        out_specs=pl.BlockSpec((tm, d), lambda i: (i, 0)),
        out_shape=jax.ShapeDtypeStruct((t, d), F32),
        scratch_shapes=[pltpu.VMEM((2, TOP_K, tm * srows, LANES), ys.dtype), pltpu.SemaphoreType.DMA((2,))],
        compiler_params=_cparams("arbitrary"),
        name="combine",
    )(dest3, dest3, gates, x1, ys, wsg, wsu, wsd, ln_g, ln_b)


def _ple_kernel(x_ref, xcol_ref, p_ref, wg_ref, bg_ref, wp_ref, gp_ref, y_ref, e_ref, *, precise):
    j = pl.program_id(1)
    tn = y_ref.shape[1]

    @pl.when(j == 0)
    def _():
        e = _dot(p_ref[...], wp_ref[...], precise)
        ms = jnp.mean(e * e, axis=-1, keepdims=True)
        e = e * lax.rsqrt(ms + RMS_EPS) * gp_ref[...]
        for jj in range(e_ref.shape[0]):
            e_ref[jj] = e[:, jj * tn:(jj + 1) * tn]

    gate = jax.nn.sigmoid(_dot(x_ref[...], wg_ref[...], precise) + bg_ref[...])
    y_ref[...] = xcol_ref[...] + gate * e_ref[j]


def _ple(x2, p2d, w_gate, b_gate, w_proj, g_proj, *, precise, tm, tn):
    t, d = x2.shape
    dp = p2d.shape[1]
    return pl.pallas_call(
        functools.partial(_ple_kernel, precise=precise),
        grid=(t // tm, d // tn),
        in_specs=[pl.BlockSpec((tm, d), lambda i, j: (i, 0)),
                  pl.BlockSpec((tm, tn), lambda i, j: (i, j)),
                  pl.BlockSpec((tm, dp), lambda i, j: (i, 0)),
                  pl.BlockSpec((d, tn), lambda i, j: (0, j)),
                  pl.BlockSpec((1, tn), lambda i, j: (0, j)),
                  pl.BlockSpec((dp, d), lambda i, j: (0, 0)),
                  pl.BlockSpec((1, d), lambda i, j: (0, 0))],
        out_specs=pl.BlockSpec((tm, tn), lambda i, j: (i, j)),
        out_shape=jax.ShapeDtypeStruct((t, d), F32),
        scratch_shapes=[pltpu.VMEM((d // tn, tm, tn), F32)],
        compiler_params=_cparams("parallel", "arbitrary"),
        name="ple",
    )(x2, x2, p2d, w_gate, b_gate, w_proj, g_proj)


def _tile(n, want):
    t = min(n, want)
    while n % t:
        t -= 1
    return t


def kernel(x_prompt, x_sample, p_prompt, p_sample, cache_k, cache_v, page_table, w_in, b_sb, sg_ln_g, sg_ln_b, w_spatial, b_spatial, g_out_sb, g_out_cm, w_out, ln1_g, ln1_b, w_router, b_router, w_exp_gate, w_exp_up, w_exp_down, w_sh_gate, w_sh_up, w_sh_down, ln2_g, ln2_b, w_ple_gate, b_ple_gate, w_ple_proj, g_ple):
    batch, seq, d = x_prompt.shape
    db, ds, _ = x_sample.shape
    depth = w_in.shape[0]
    n_pool, page, heads, dh = cache_k.shape[1:]
    d_sb = heads * dh
    d_cm = sg_ln_g.shape[1]
    groups, chunk = w_spatial.shape[1], w_spatial.shape[2]
    cg = d_cm // groups
    n_e = w_router.shape[2]
    alpha = float((2 * depth) ** 0.25)
    tp, ts = batch * seq, db * ds
    assert d_sb == d_cm and w_in.shape[2] == 3 * d_sb + 2 * d_cm
    assert seq % chunk == 0 and ds <= min(chunk, SUBLANES) and page == chunk == LANES
    assert d % (2 * LANES) == 0 and dh == LANES
    srows = d // LANES

    row = lambda a: a.reshape(1, -1)
    tril = jnp.tril(jnp.ones((chunk, chunk), jnp.bool_))
    xp = x_prompt.reshape(tp, d)
    xs = x_sample.reshape(ts, d)
    outs = {k: [] for k in ("kp", "vp", "ks", "vs", "cs")}

    for li in range(depth):
        lng, lnb = row(sg_ln_g[li]), row(sg_ln_b[li])
        w_mix = jnp.where(tril[None], w_spatial[li], 0.0)
        b_mix = jnp.broadcast_to(b_spatial[li][:, :, None], (groups, chunk, cg))
        w_mix_s = jnp.einsum("ab,gts->gatbs", jnp.eye(db, dtype=F32), w_mix[:, :ds, :ds]).reshape(groups, ts, ts)
        b_mix_s = jnp.tile(b_mix[:, :ds], (1, db, 1))
        wr_t = w_router[li].T
        br = jnp.broadcast_to(b_router[li][:, None], (n_e, 256))

        q_scale = LOG2E * dh ** -0.5

        q, k, v, u_act, vn, k_b, v_b = _proj_in(
            xp, w_in[li].astype(BF16), lng, lnb, width=d_sb, group=cg, precise=False, tm=_tile(tp, 512),
            tn=_tile(d_sb, 512), dtypes=(BF16, F32, F32, BF16, BF16, BF16, BF16), q_scale=q_scale)
        o_sb = _attn_prompt(q, k_b, v_b, b_sb[li], batch=batch, seq=seq, heads=heads, tq=_tile(seq, 256),
                            hp=_tile(heads, ATTN_HEADS_PER_STEP), out_dtype=BF16)
        x1p, xpk_p = _merge(o_sb, u_act, vn, xp, w_mix.astype(BF16), b_mix, row(g_out_sb[li]),
                            row(g_out_cm[li]), w_out[li].astype(BF16), row(ln1_g[li]), row(ln1_b[li]),
                            precise=False, alpha=alpha, heads=heads, groups=groups, tm=_tile(tp, 256))
        cnt0 = jnp.zeros((n_e, LANES), F32)
        idx_p, gate_p, rank_p, cnt_p = _router(x1p, wr_t.astype(BF16), br, cnt0, precise=False,
                                               tm=_tile(tp, 256))

        qs, ks, vs, us, vns, _, _ = _proj_in(
            xs, w_in[li], lng, lnb, width=d_sb, group=cg, precise=True, tm=ts, tn=_tile(d_sb, 512),
            dtypes=(F32,) * 7, q_scale=q_scale)
        q_rows = jnp.pad(qs.reshape(db, ds, heads, dh).transpose(0, 2, 1, 3),
                         ((0, 0), (0, 0), (0, SUBLANES - ds), (0, 0))).reshape(db, heads * SUBLANES, dh)
        pad_new = lambda a: jnp.pad(a.reshape(db, ds, heads, dh),
                                    ((0, 0), (0, page - ds), (0, 0), (0, 0))).reshape(db, page * heads, dh)
        o_rows = _attn_decode(q_rows, pad_new(ks), pad_new(vs),
                              cache_k[li].reshape(n_pool, page * heads, dh),
                              cache_v[li].reshape(n_pool, page * heads, dh),
                              page_table, b_sb[li], heads=heads, rows=SUBLANES, page=page,
                              pps=_tile(page_table.shape[1], DECODE_PAGES_PER_STEP))
        o_sb_s = o_rows.reshape(db, heads, SUBLANES, dh)[:, :, :ds].transpose(0, 2, 1, 3).reshape(ts, d_sb)
        x1s, xpk_s = _merge(o_sb_s, us, vns, xs, w_mix_s, b_mix_s, row(g_out_sb[li]), row(g_out_cm[li]),
                            w_out[li], row(ln1_g[li]), row(ln1_b[li]), precise=True, alpha=alpha, heads=heads,
                            groups=groups, tm=ts)
        idx_s, gate_s, rank_s, cnt = _router(x1s, wr_t, br, cnt_p, precise=True, tm=ts)

        blk = EXPERT_ROW_BLOCK
        n_rows = ((tp + ts) * TOP_K + n_e * (blk - 1) + blk - 1) // blk * blk
        tm_c = _tile(tp, 128)
        steps = lambda dest, tm: dest.reshape(TOP_K, -1, tm).transpose(1, 0, 2)
        dest_p, _, _ = _plan(idx_p, rank_p, cnt, blk=blk, n_blocks=n_rows // blk, tm=_tile(tp, 256))
        dest_s, blk_e, n_used = _plan(idx_s, rank_s, cnt, blk=blk, n_blocks=n_rows // blk, tm=ts)
        dest_p, dest_s = steps(dest_p, tm_c), steps(dest_s, ts)
        rows_x = _scatter_rows(dest_p, xpk_p, None, n_rows=n_rows, srows=srows, tm=tm_c)
        rows_x = _scatter_rows(dest_s, xpk_s, rows_x, n_rows=n_rows, srows=srows, tm=ts)
        rows_y = _experts(rows_x, blk_e[0], n_used[0, :1], w_exp_gate, w_exp_up, w_exp_down, li,
                          blk=blk, srows=srows)

        x2p = _combine(dest_p, gate_p.T, x1p, rows_y, w_sh_gate[li].astype(BF16), w_sh_up[li].astype(BF16),
                       w_sh_down[li].astype(BF16), row(ln2_g[li]), row(ln2_b[li]), precise=False, alpha=alpha,
                       srows=srows)
        xp = _ple(x2p, p_prompt[li].reshape(tp, -1), w_ple_gate[li].astype(BF16), row(b_ple_gate[li]),
                  w_ple_proj[li].astype(BF16), row(g_ple[li]), precise=False, tm=_tile(tp, 512),
                  tn=_tile(d, 512))
        x2s = _combine(dest_s, gate_s.T, x1s, rows_y, w_sh_gate[li], w_sh_up[li], w_sh_down[li],
                       row(ln2_g[li]), row(ln2_b[li]), precise=True, alpha=alpha, srows=srows)
        xs = _ple(x2s, p_sample[li].reshape(ts, -1), w_ple_gate[li], row(b_ple_gate[li]), w_ple_proj[li],
                  row(g_ple[li]), precise=True, tm=ts, tn=_tile(d, 512))

        outs["kp"].append(k.reshape(batch, seq, heads, dh))
        outs["vp"].append(v.reshape(batch, seq, heads, dh))
        outs["ks"].append(ks.reshape(db, ds, heads, dh))
        outs["vs"].append(vs.reshape(db, ds, heads, dh))
        outs["cs"].append(vns.reshape(db, ds, groups, cg))

    return (xp.reshape(batch, seq, d), xs.reshape(db, ds, d),
            jnp.stack(outs["kp"], 0), jnp.stack(outs["vp"], 0), jnp.stack(outs["ks"], 0),
            jnp.stack(outs["vs"], 0), jnp.stack(outs["cs"], 0))
```
